```python
import jax, jax.numpy as jnp
from jax import lax
import numpy as np

D_MODEL = 1024
BATCH = 16
SEQ = 2048
DEPTH = 4

GRID_W = 64
CTX_LEN = 256
N_MIXERS = 2
N_A_LAYERS = (DEPTH + 1) // 2
N_B_LAYERS = DEPTH // 2
CHUNK = 64
EPS = 1e-6
N_MOD = 6
A_HEADS = 8
A_DK = D_MODEL // 16
A_DV = D_MODEL // 8
A_PROJ = 2 * A_HEADS * A_DK + 2 * A_HEADS * A_DV + 4 * A_HEADS
CONV_W = 5
B_HEADS = 8
B_DK = D_MODEL // B_HEADS
B_DV = D_MODEL // B_HEADS
B_PROJ = 2 * B_HEADS * B_DK + 2 * B_HEADS * B_DV
ROPE_BASE = 10000.0
N_EXPERTS = 16
EXPERT_FF = D_MODEL
CAPACITY_FACTOR = 2

kernel_name = "hybrid_mlstm_retention_ec_moe_diffusion"


def rmsnorm(x, g):
    xf = x.astype(jnp.float32)
    y = xf * lax.rsqrt(jnp.mean(xf * xf, axis=-1, keepdims=True) + EPS)
    return (y * g.astype(jnp.float32)).astype(x.dtype)


def modulate(h, shift, scale):
    return h * (1 + scale) + shift


def heads(a, n_heads):
    b, t, _ = a.shape
    return a.reshape(b, t, n_heads, -1).transpose(0, 2, 1, 3)


def dwconv_centred(x, w, b):
    pad = CONV_W // 2
    y = lax.conv_general_dilated(x, w[:, None, :].astype(x.dtype), window_strides=(1,),
                                 padding=[(pad, pad)], dimension_numbers=('NWC', 'WIO', 'NWC'),
                                 feature_group_count=x.shape[-1])
    return y + b.astype(x.dtype)


def to_chunks(a):
    b, h, t = a.shape[:3]
    return jnp.moveaxis(a.reshape(b, h, t // CHUNK, CHUNK, *a.shape[3:]), 2, 0)


def from_chunks(a):
    nc, b, h, l = a.shape[:4]
    return jnp.moveaxis(a, 0, 2).reshape(b, h, nc * l, *a.shape[4:])


def mlstm_scan(q, k, v, ig, lf, state):
    tril = jnp.tril(jnp.ones((CHUNK, CHUNK), bool))

    def step(carry, inp):
        C, n, m = carry
        qc, kc, vc, ic, fc = inp
        b = jnp.cumsum(fc, axis=-1)
        dmat = jnp.where(tril, b[..., :, None] - b[..., None, :] + ic[..., None, :], -jnp.inf)
        inter = b + m[..., None]
        m_row = jnp.maximum(inter, dmat.max(-1))
        s = jnp.einsum('bhid,bhjd->bhij', qc, kc) * jnp.exp(dmat - m_row[..., None])
        w_inter = jnp.exp(inter - m_row)
        num = jnp.einsum('bhij,bhjv->bhiv', s, vc) + w_inter[..., None] * jnp.einsum('bhid,bhdv->bhiv', qc, C)
        den = s.sum(-1) + w_inter * jnp.einsum('bhid,bhd->bhi', qc, n)
        h = num / jnp.maximum(jnp.abs(den), jnp.exp(-m_row))[..., None]
        b_last = b[..., -1]
        gk = b_last[..., None] - b + ic
        m_new = jnp.maximum(b_last + m, gk.max(-1))
        wk = jnp.exp(gk - m_new[..., None])
        wc = jnp.exp(b_last + m - m_new)
        C_new = wc[..., None, None] * C + jnp.einsum('bhj,bhjd,bhjv->bhdv', wk, kc, vc)
        n_new = wc[..., None] * n + jnp.einsum('bhj,bhjd->bhd', wk, kc)
        return (C_new, n_new, m_new), h

    xs = tuple(to_chunks(a) for a in (q, k, v, ig, lf))
    state, h = lax.scan(step, state, xs)
    return from_chunks(h), state


def retention_scan(q, k, v, logg, S):
    idx = jnp.arange(CHUNK, dtype=jnp.float32)
    tril = jnp.tril(jnp.ones((CHUNK, CHUNK), bool))
    diff = jnp.where(tril, idx[:, None] - idx[None, :], 0.0)
    dmat = jnp.where(tril, jnp.exp(diff * logg[:, None, None]), 0.0)
    inter = jnp.exp((idx + 1.0) * logg[:, None])
    kdec = jnp.exp((CHUNK - 1.0 - idx) * logg[:, None])
    sdec = jnp.exp(CHUNK * logg)

    def step(S, inp):
        qc, kc, vc = inp
        o = jnp.einsum('bhij,bhjv->bhiv', jnp.einsum('bhid,bhjd->bhij', qc, kc) * dmat, vc) \
            + inter[..., None] * jnp.einsum('bhid,bhdv->bhiv', qc, S)
        S_new = sdec[:, None, None] * S + jnp.einsum('hj,bhjd,bhjv->bhdv', kdec, kc, vc)
        return S_new, o

    xs = tuple(to_chunks(a) for a in (q, k, v))
    S, o = lax.scan(step, S, xs)
    return from_chunks(o), S


def run_bidirectional(scan_f, scan_b, ctx_f, lat_f, ctx_b, lat_b, state0):
    flip = lambda t: tuple(jnp.flip(a, 2) for a in t)
    oc_f, s = scan_f(ctx_f, state0)
    ol_f, _ = scan_f(lat_f, s)
    oc_b, s = scan_b(flip(ctx_b), state0)
    ol_b, _ = scan_b(flip(lat_b), s)
    return oc_f + jnp.flip(oc_b, 2), ol_f + jnp.flip(ol_b, 2)


def mlstm_mixer(h_ctx, h_lat, w_in, conv_w, conv_b, ig_b, fg_b, norm_g, w_out, need_ctx_out):
    nq = A_HEADS * A_DK
    nv = A_HEADS * A_DV
    f32 = jnp.float32

    def project(h):
        bn, t, _ = h.shape
        p = h @ w_in
        qk = jax.nn.silu(dwconv_centred(p[..., :2 * nq], conv_w, conv_b)).astype(f32)
        q = heads(qk[..., :nq], A_HEADS) * (A_DK ** -0.5)
        k = heads(qk[..., nq:], A_HEADS)
        v = heads(p[..., 2 * nq:2 * nq + nv].astype(f32), A_HEADS)
        o = p[..., 2 * nq + nv:2 * nq + 2 * nv]
        g = p[..., 2 * nq + 2 * nv:].astype(f32)
        ig = g[..., :2 * A_HEADS].reshape(bn, t, 2, A_HEADS) + ig_b.astype(f32)
        lf = jax.nn.log_sigmoid(g[..., 2 * A_HEADS:].reshape(bn, t, 2, A_HEADS) + fg_b.astype(f32))
        ig = jnp.transpose(ig, (0, 2, 3, 1))
        lf = jnp.transpose(lf, (0, 2, 3, 1))
        fwd = (q, k, v, ig[:, 0], lf[:, 0])
        bwd = (q, k, v, ig[:, 1], lf[:, 1])
        return fwd, bwd, o

    cf, cb, o_ctx = project(h_ctx)
    lf_, lb, o_lat = project(h_lat)
    bn = h_lat.shape[0]
    state0 = (jnp.zeros((bn, A_HEADS, A_DK, A_DV), f32), jnp.zeros((bn, A_HEADS, A_DK), f32),
              jnp.zeros((bn, A_HEADS), f32))
    scan = lambda args, s: mlstm_scan(*args, s)
    y_ctx, y_lat = run_bidirectional(scan, scan, cf, lf_, cb, lb, state0)

    def finish(y, o):
        b_, h_, t_, d_ = y.shape
        y = jnp.transpose(y, (0, 2, 1, 3))
        y = y * lax.rsqrt(jnp.mean(y * y, axis=-1, keepdims=True) + EPS)
        y = y.reshape(b_, t_, h_ * d_) * norm_g.astype(f32) * jax.nn.sigmoid(o.astype(f32))
        return y.astype(o.dtype) @ w_out

    out_ctx = finish(y_ctx, o_ctx) if need_ctx_out else None
    return out_ctx, finish(y_lat, o_lat)


def rope_angles(t):
    rows = t // GRID_W
    r = jnp.broadcast_to(jnp.arange(rows, dtype=jnp.float32)[:, None], (rows, GRID_W)).reshape(-1)
    col = jnp.broadcast_to(jnp.arange(GRID_W, dtype=jnp.float32)[None, :], (rows, GRID_W)).reshape(-1)
    nf = B_DK // 4
    inv = ROPE_BASE ** (-jnp.arange(nf, dtype=jnp.float32) / nf)
    return r[:, None] * inv, col[:, None] * inv


def rope(x, ang):
    x1, x2 = jnp.split(x, 2, axis=-1)
    cos, sin = jnp.cos(ang), jnp.sin(ang)
    return jnp.concatenate([x1 * cos - x2 * sin, x1 * sin + x2 * cos], axis=-1)


def rope2d(x, ang_r, ang_c):
    half = B_DK // 2
    return jnp.concatenate([rope(x[..., :half], ang_r), rope(x[..., half:], ang_c)], axis=-1)


def retention_mixer(h_ctx, h_lat, w_in, decay_logit, gn_g, w_out, need_ctx_out):
    n = B_HEADS * B_DK
    f32 = jnp.float32

    def project(h, angles):
        p = h @ w_in
        q = heads(p[..., :n].astype(f32), B_HEADS)
        k = heads(p[..., n:2 * n].astype(f32), B_HEADS) * (B_DK ** -0.5)
        v = heads(p[..., 2 * n:3 * n].astype(f32), B_HEADS)
        g = p[..., 3 * n:]
        if angles is not None:
            q = rope2d(q, *angles)
            k = rope2d(k, *angles)
        return (q, k, v), g

    c_args, g_ctx = project(h_ctx, None)
    l_args, g_lat = project(h_lat, rope_angles(h_lat.shape[1]))
    logg = jax.nn.log_sigmoid(decay_logit.astype(f32))
    bn = h_lat.shape[0]
    state0 = jnp.zeros((bn, B_HEADS, B_DK, B_DV), f32)
    scan_f = lambda args, s: retention_scan(*args, logg[0], s)
    scan_b = lambda args, s: retention_scan(*args, logg[1], s)
    y_ctx, y_lat = run_bidirectional(scan_f, scan_b, c_args, l_args, c_args, l_args, state0)

    def finish(y, g):
        b_, h_, t_, d_ = y.shape
        y = jnp.transpose(y, (0, 2, 1, 3))
        mu = jnp.mean(y, axis=-1, keepdims=True)
        var = jnp.mean(jnp.square(y - mu), axis=-1, keepdims=True)
        y = ((y - mu) * lax.rsqrt(var + EPS)).reshape(b_, t_, h_ * d_) * gn_g.astype(f32)
        y = y * jax.nn.silu(g.astype(f32))
        return y.astype(g.dtype) @ w_out

    out_ctx = finish(y_ctx, g_ctx) if need_ctx_out else None
    return out_ctx, finish(y_lat, g_lat)


def expert_choice_ffn(h, router, w_gate, w_up, w_down):
    bn, t, _ = h.shape
    cap = CAPACITY_FACTOR * t // N_EXPERTS
    aff = jax.nn.softmax(h.astype(jnp.float32) @ router.astype(jnp.float32), axis=-1)
    gate, idx = lax.top_k(jnp.swapaxes(aff, 1, 2), cap)
    bidx = jnp.arange(bn)[:, None, None]
    xs = h[bidx, idx]
    hid = jax.nn.silu(jnp.einsum('becd,edf->becf', xs, w_gate)) * jnp.einsum('becd,edf->becf', xs, w_up)
    y = jnp.einsum('becf,efd->becd', hid, w_down) * gate[..., None].astype(h.dtype)
    return jnp.zeros_like(h).at[bidx, idx].add(y)


def setup_inputs(seed: int = 0) -> dict:
    key = jax.random.key(seed)
    ks = jax.random.split(key, 24)
    f32 = jnp.float32
    D = D_MODEL
    nqk = 2 * A_HEADS * A_DK
    nrm = lambda k, shape, s: jax.random.normal(k, shape, f32) * s
    fbase = jnp.linspace(3.0, 6.0, A_HEADS, dtype=f32)
    dbase = jnp.log(2.0 ** (5.0 + jnp.arange(B_HEADS, dtype=f32)) - 1.0)
    return {
        "x": nrm(ks[0], (BATCH, SEQ, D), 1.0),
        "c": nrm(ks[1], (BATCH, D), 1.0),
        "ctx": nrm(ks[2], (BATCH, CTX_LEN, D), 1.0),
        "c_ctx": nrm(ks[3], (D,), 1.0),
        "ada_w": nrm(ks[4], (DEPTH, D, N_MOD * D), 0.5 * D ** -0.5),
        "ada_b": nrm(ks[5], (DEPTH, N_MOD * D), 0.02),
        "norm_mix_g": 1.0 + nrm(ks[6], (DEPTH, D), 0.02),
        "norm_ffn_g": 1.0 + nrm(ks[7], (DEPTH, D), 0.02),
        "final_norm_g": 1.0 + nrm(ks[8], (D,), 0.02),
        "mlstm_w_in": nrm(ks[9], (N_A_LAYERS, D, A_PROJ), D ** -0.5),
        "mlstm_conv_w": nrm(ks[10], (N_A_LAYERS, CONV_W, nqk), CONV_W ** -0.5),
        "mlstm_conv_b": nrm(ks[11], (N_A_LAYERS, nqk), 0.02),
        "mlstm_igate_b": nrm(ks[12], (N_A_LAYERS, 2, A_HEADS), 0.1),
        "mlstm_fgate_b": fbase + nrm(ks[13], (N_A_LAYERS, 2, A_HEADS), 0.1),
        "mlstm_head_norm_g": 1.0 + nrm(ks[14], (N_A_LAYERS, A_HEADS * A_DV), 0.02),
        "mlstm_w_out": nrm(ks[15], (N_A_LAYERS, A_HEADS * A_DV, D), (A_HEADS * A_DV) ** -0.5),
        "ret_w_in": nrm(ks[16], (N_B_LAYERS, D, B_PROJ), D ** -0.5),
        "ret_decay_logit": dbase + nrm(ks[17], (N_B_LAYERS, 2, B_HEADS), 0.1),
        "ret_group_norm_g": 1.0 + nrm(ks[18], (N_B_LAYERS, B_HEADS * B_DV), 0.02),
        "ret_w_out": nrm(ks[19], (N_B_LAYERS, B_HEADS * B_DV, D), (B_HEADS * B_DV) ** -0.5),
        "moe_router": nrm(ks[20], (DEPTH, D, N_EXPERTS), D ** -0.5),
        "moe_w_gate": nrm(ks[21], (DEPTH, N_EXPERTS, D, EXPERT_FF), D ** -0.5),
        "moe_w_up": nrm(ks[22], (DEPTH, N_EXPERTS, D, EXPERT_FF), D ** -0.5),
        "moe_w_down": nrm(ks[23], (DEPTH, N_EXPERTS, EXPERT_FF, D), EXPERT_FF ** -0.5),
    }


def reference(x, c, ctx, c_ctx, ada_w, ada_b, norm_mix_g, norm_ffn_g, final_norm_g,
              mlstm_w_in, mlstm_conv_w, mlstm_conv_b, mlstm_igate_b, mlstm_fgate_b,
              mlstm_head_norm_g, mlstm_w_out, ret_w_in, ret_decay_logit, ret_group_norm_g,
              ret_w_out, moe_router, moe_w_gate, moe_w_up, moe_w_down):
    c_lat = jax.nn.silu(c)[:, None, :]
    c_con = jax.nn.silu(c_ctx)[None, None, :]
    for i in range(DEPTH):
        last = i == DEPTH - 1
        mod_l = jnp.split(c_lat @ ada_w[i] + ada_b[i], N_MOD, axis=-1)
        mod_c = jnp.split(c_con @ ada_w[i] + ada_b[i], N_MOD, axis=-1)
        h_lat = modulate(rmsnorm(x, norm_mix_g[i]), mod_l[0], mod_l[1])
        h_ctx = modulate(rmsnorm(ctx, norm_mix_g[i]), mod_c[0], mod_c[1])
        j = i // N_MIXERS
        if i % N_MIXERS == 0:
            o_ctx, o_lat = mlstm_mixer(h_ctx, h_lat, mlstm_w_in[j], mlstm_conv_w[j], mlstm_conv_b[j],
                                       mlstm_igate_b[j], mlstm_fgate_b[j], mlstm_head_norm_g[j],
                                       mlstm_w_out[j], not last)
        else:
            o_ctx, o_lat = retention_mixer(h_ctx, h_lat, ret_w_in[j], ret_decay_logit[j],
                                           ret_group_norm_g[j], ret_w_out[j], not last)
        x = x + mod_l[2] * o_lat
        h_lat = modulate(rmsnorm(x, norm_ffn_g[i]), mod_l[3], mod_l[4])
        x = x + mod_l[5] * expert_choice_ffn(h_lat, moe_router[i], moe_w_gate[i], moe_w_up[i], moe_w_down[i])
        if not last:
            ctx = ctx + mod_c[2] * o_ctx
            h_ctx = modulate(rmsnorm(ctx, norm_ffn_g[i]), mod_c[3], mod_c[4])
            ctx = ctx + mod_c[5] * expert_choice_ffn(h_ctx, moe_router[i], moe_w_gate[i], moe_w_up[i], moe_w_down[i])
    return rmsnorm(x, final_norm_g)
```

```python
import functools
import math

import jax
import jax.numpy as jnp
from jax import lax
from jax.experimental import pallas as pl
from jax.experimental.pallas import tpu as pltpu

F32 = jnp.float32
BF16 = jnp.bfloat16
I32 = jnp.int32
HIGHEST = lax.Precision.HIGHEST

EPS = 1e-6
N_MOD = 6
CHUNK = 256
TOK_TILE = 768
A_HEADS, A_DK, A_DV = 8, 64, 128
B_HEADS, B_DK, B_DV = 8, 128, 128
CONV_W = 5
N_EXPERTS = 16
CAPACITY_FACTOR = 2
GRID_W = 64
ROPE_BASE = 10000.0
VMEM_LIMIT = 56 * 1024 * 1024


def _params(*sem):
    return pltpu.CompilerParams(dimension_semantics=sem, vmem_limit_bytes=VMEM_LIMIT)


def _sigmoid(x):
    return 1.0 / (1.0 + jnp.exp(-x))


def _log_sigmoid(x):
    return jnp.minimum(x, 0.0) - jnp.log1p(jnp.exp(-jnp.abs(x)))


def _nt_dot(a, b, **kw):
    return lax.dot_general(a, b, (((1,), (1,)), ((), ())), preferred_element_type=F32, **kw)


def _dot(a, b, **kw):
    return jnp.dot(a, b, preferred_element_type=F32, **kw)


def _rms_mod(x, g, shift, scale):
    y = x * lax.rsqrt(jnp.mean(x * x, axis=-1, keepdims=True) + EPS)
    return (y * g) * (1.0 + scale) + shift


def _row_mod(mod_l_ref, mod_c_ref, k, is_ctx):
    return jnp.where(is_ctx, mod_c_ref[k:k + 1, :], mod_l_ref[k:k + 1, :])


def _is_ctx_rows(tile, rows):
    r = tile * rows + lax.broadcasted_iota(I32, (rows, 1), 0)
    return r < CHUNK


def _adaln_kernel(c_ref, w_ref, b_ref, o_ref):
    c = c_ref[...]
    s = c * _sigmoid(c)
    o_ref[...] = _dot(s, w_ref[...], precision=HIGHEST) + b_ref[...]


def _adaln(cvec, ada_w, ada_b):
    depth, d, n = ada_w.shape
    rows = cvec.shape[0]
    tn = 1536
    return pl.pallas_call(
        _adaln_kernel,
        grid=(depth, n // tn),
        in_specs=[pl.BlockSpec((rows, d), lambda l, j: (0, 0)),
                  pl.BlockSpec((None, d, tn), lambda l, j: (l, 0, j)),
                  pl.BlockSpec((None, 1, tn), lambda l, j: (l, 0, j))],
        out_specs=pl.BlockSpec((None, rows, tn), lambda l, j: (l, 0, j)),
        out_shape=jax.ShapeDtypeStruct((depth, rows, n), F32),
        compiler_params=_params("parallel", "parallel"),
        name="adaln",
    )(cvec, ada_w, ada_b.reshape(depth, 1, n))


def _inproj_a_kernel(x_ref, g_ref, ml_ref, mc_ref, wq_ref, wkt_ref, wv_ref, wo_ref, wgi_ref, wgf_ref,
                     q_ref, kt_ref, v_ref, o_ref, gi_ref, gf_ref):
    is_ctx = _is_ctx_rows(pl.program_id(1), x_ref.shape[0])
    h = _rms_mod(x_ref[...], g_ref[...], _row_mod(ml_ref, mc_ref, 0, is_ctx),
                 _row_mod(ml_ref, mc_ref, 1, is_ctx)).astype(BF16)
    q_ref[...] = _dot(h, wq_ref[...]).astype(BF16)
    kt_ref[...] = _nt_dot(wkt_ref[...], h).astype(BF16)
    v_ref[...] = _dot(h, wv_ref[...]).astype(BF16)
    o_ref[...] = _dot(h, wo_ref[...]).astype(BF16)
    gi_ref[...] = _dot(h, wgi_ref[...])
    gf_ref[...] = _dot(h, wgf_ref[...])


def _inproj_b_kernel(x_ref, g_ref, ml_ref, mc_ref, cs_ref, sn_ref, cst_ref, snt_ref,
                     wq_ref, wkt_ref, wv_ref, wg_ref, q_ref, kt_ref, v_ref, gate_ref):
    is_ctx = _is_ctx_rows(pl.program_id(1), x_ref.shape[0])
    h = _rms_mod(x_ref[...], g_ref[...], _row_mod(ml_ref, mc_ref, 0, is_ctx),
                 _row_mod(ml_ref, mc_ref, 1, is_ctx)).astype(BF16)
    v_ref[...] = _dot(h, wv_ref[...]).astype(BF16)
    gate_ref[...] = _dot(h, wg_ref[...]).astype(BF16)
    cs, sn = cs_ref[...], sn_ref[...]
    lane = lax.broadcasted_iota(I32, cs.shape, 1)
    first = (lane % 64) < 32
    q = _dot(h, wq_ref[...])
    for hd in range(B_HEADS):
        qh = q[:, hd * B_DK:(hd + 1) * B_DK]
        partner = jnp.where(first, pltpu.roll(qh, B_DK - 32, axis=1), pltpu.roll(qh, 32, axis=1))
        q_ref[:, hd * B_DK:(hd + 1) * B_DK] = (qh * cs + partner * sn).astype(BF16)
    kt = _nt_dot(wkt_ref[...], h) * (B_DK ** -0.5)
    cst, snt = cst_ref[...], snt_ref[...]
    for hd in range(B_HEADS):
        for half in range(2):
            base = hd * B_DK + half * 64
            x1 = kt[base:base + 32, :]
            x2 = kt[base + 32:base + 64, :]
            c_ = cst[half * 32:(half + 1) * 32, :]
            s_ = snt[half * 32:(half + 1) * 32, :]
            kt_ref[base:base + 32, :] = (x1 * c_ - x2 * s_).astype(BF16)
            kt_ref[base + 32:base + 64, :] = (x1 * s_ + x2 * c_).astype(BF16)


def _tok_spec(cols, tm):
    return pl.BlockSpec((None, tm, cols), lambda b, t: (b, t, 0))


def _full_spec(shape):
    nd = len(shape)
    return pl.BlockSpec(shape, lambda b, t: (0,) * nd, pipeline_mode=pl.Buffered(1))


def _mod_specs(d):
    return [pl.BlockSpec((None, N_MOD, d), lambda b, t: (b, 0, 0)), _full_spec((N_MOD, d))]


def _inproj_a(xall, g, mod_l, mod_c, w_in):
    bn, tt, d = xall.shape
    tm = TOK_TILE
    nq, nv = A_HEADS * A_DK, A_HEADS * A_DV
    wq = w_in[:, :nq].astype(BF16)
    wkt = w_in[:, nq:2 * nq].T.astype(BF16)
    wv = w_in[:, 2 * nq:2 * nq + nv].astype(BF16)
    wo = w_in[:, 2 * nq + nv:2 * nq + 2 * nv].astype(BF16)
    wg = w_in[:, 2 * nq + 2 * nv:]
    pad = jnp.zeros((d, 128 - 2 * A_HEADS), F32)
    perm = jnp.array(_GATE_PERM)
    wgi = jnp.concatenate([wg[:, :2 * A_HEADS][:, perm], pad], axis=1).astype(BF16)
    wgf = jnp.concatenate([wg[:, 2 * A_HEADS:][:, perm], pad], axis=1).astype(BF16)
    ws = [wq, wkt, wv, wo, wgi, wgf]
    return pl.pallas_call(
        _inproj_a_kernel,
        grid=(bn, tt // tm),
        in_specs=[_tok_spec(d, tm), _full_spec((1, d))] + _mod_specs(d) + [_full_spec(w.shape) for w in ws],
        out_specs=[_tok_spec(nq, tm), pl.BlockSpec((None, nq, tm), lambda b, t: (b, 0, t)),
                   _tok_spec(nv, tm), _tok_spec(nv, tm), _tok_spec(128, tm), _tok_spec(128, tm)],
        out_shape=[jax.ShapeDtypeStruct((bn, tt, nq), BF16), jax.ShapeDtypeStruct((bn, nq, tt), BF16),
                   jax.ShapeDtypeStruct((bn, tt, nv), BF16), jax.ShapeDtypeStruct((bn, tt, nv), BF16),
                   jax.ShapeDtypeStruct((bn, tt, 128), F32), jax.ShapeDtypeStruct((bn, tt, 128), F32)],
        compiler_params=_params("parallel", "parallel"),
        name="inproj_mlstm",
    )(xall, g.reshape(1, d), mod_l, mod_c, *ws)


def _rope_tables(tt):
    seq = tt - CHUNK
    rows = seq // GRID_W
    r = jnp.broadcast_to(jnp.arange(rows, dtype=F32)[:, None], (rows, GRID_W)).reshape(-1)
    col = jnp.broadcast_to(jnp.arange(GRID_W, dtype=F32)[None, :], (rows, GRID_W)).reshape(-1)
    nf = B_DK // 4
    inv = ROPE_BASE ** (-jnp.arange(nf, dtype=F32) / nf)
    ang = jnp.concatenate([r[:, None] * inv, col[:, None] * inv], axis=1)
    cos = jnp.concatenate([jnp.ones((CHUNK, 2 * nf), F32), jnp.cos(ang)], axis=0)
    sin = jnp.concatenate([jnp.zeros((CHUNK, 2 * nf), F32), jnp.sin(ang)], axis=0)
    cr, cc, sr, sc = cos[:, :nf], cos[:, nf:], sin[:, :nf], sin[:, nf:]
    cs = jnp.concatenate([cr, cr, cc, cc], axis=1)
    sn = jnp.concatenate([-sr, sr, -sc, sc], axis=1)
    return cs, sn, cos.T, sin.T


def _inproj_b(xall, g, mod_l, mod_c, w_in):
    bn, tt, d = xall.shape
    tm = TOK_TILE
    n = B_HEADS * B_DK
    wq = w_in[:, :n].astype(BF16)
    wkt = w_in[:, n:2 * n].T.astype(BF16)
    wv = w_in[:, 2 * n:3 * n].astype(BF16)
    wg = w_in[:, 3 * n:].astype(BF16)
    cs, sn, cst, snt = _rope_tables(tt)
    ws = [wq, wkt, wv, wg]
    tab_specs = [pl.BlockSpec((tm, B_DK), lambda b, t: (t, 0)), pl.BlockSpec((tm, B_DK), lambda b, t: (t, 0)),
                 pl.BlockSpec((64, tm), lambda b, t: (0, t)), pl.BlockSpec((64, tm), lambda b, t: (0, t))]
    return pl.pallas_call(
        _inproj_b_kernel,
        grid=(bn, tt // tm),
        in_specs=[_tok_spec(d, tm), _full_spec((1, d))] + _mod_specs(d) + tab_specs
                 + [_full_spec(w.shape) for w in ws],
        out_specs=[_tok_spec(n, tm), pl.BlockSpec((None, n, tm), lambda b, t: (b, 0, t)),
                   _tok_spec(n, tm), _tok_spec(n, tm)],
        out_shape=[jax.ShapeDtypeStruct((bn, tt, n), BF16), jax.ShapeDtypeStruct((bn, n, tt), BF16),
                   jax.ShapeDtypeStruct((bn, tt, n), BF16), jax.ShapeDtypeStruct((bn, tt, n), BF16)],
        compiler_params=_params("parallel", "parallel"),
        name="inproj_ret",
    )(xall, g.reshape(1, d), mod_l, mod_c, cs, sn, cst, snt, *ws)


_GATE_PERM = [d * A_HEADS + 2 * hp + hh for hp in range(A_HEADS // 2) for d in range(2) for hh in range(2)]


def _chunk_scan(x, pos, op, reverse):
    n = x.shape[0]
    s = 1
    while s < CHUNK:
        if reverse:
            sh = pltpu.roll(x, n - s, axis=0)
            ok = pos < CHUNK - s
        else:
            sh = pltpu.roll(x, s, axis=0)
            ok = pos >= s
        x = jnp.where(ok, op(x, sh), x)
        s *= 2
    return x


def _mlstm_gates_kernel(gi_ref, gf_ref, bi_ref, bf_ref, b_ref, m_ref, at_ref):
    ig = gi_ref[...] + bi_ref[...]
    lf = _log_sigmoid(gf_ref[...] + bf_ref[...])
    tt = ig.shape[0]
    pos = lax.broadcasted_iota(I32, (tt, 1), 0) % CHUNK
    fwd = (lax.broadcasted_iota(I32, (1, ig.shape[1]), 1) // 2) % 2 == 0
    b = jnp.where(fwd, _chunk_scan(lf, pos, jnp.add, False), _chunk_scan(lf, pos, jnp.add, True))
    a = ig - b
    mloc = jnp.where(fwd, _chunk_scan(a, pos, jnp.maximum, False), _chunk_scan(a, pos, jnp.maximum, True))
    for hp in range(A_HEADS // 2):
        b_ref[hp] = b[:, 4 * hp:4 * hp + 4]
        m_ref[hp] = mloc[:, 4 * hp:4 * hp + 4]
    at_ref[...] = a.T[:2 * A_HEADS, :]


def _mlstm_gates(gi, gf, ig_b, fg_b):
    bn, tt, w = gi.shape
    nc = 2 * A_HEADS
    perm = jnp.array(_GATE_PERM)
    padrow = lambda v: jnp.concatenate([v.reshape(nc)[perm].reshape(1, nc).astype(F32),
                                        jnp.zeros((1, w - nc), F32)], axis=1)
    nhp = A_HEADS // 2
    spec = pl.BlockSpec((None, tt, w), lambda b: (b, 0, 0))
    row = pl.BlockSpec((1, w), lambda b: (0, 0))
    return pl.pallas_call(
        _mlstm_gates_kernel,
        grid=(bn,),
        in_specs=[spec, spec, row, row],
        out_specs=[pl.BlockSpec((None, nhp, tt, 4), lambda b: (b, 0, 0, 0)),
                   pl.BlockSpec((None, nhp, tt, 4), lambda b: (b, 0, 0, 0)),
                   pl.BlockSpec((None, nc, tt), lambda b: (b, 0, 0))],
        out_shape=[jax.ShapeDtypeStruct((bn, nhp, tt, 4), F32), jax.ShapeDtypeStruct((bn, nhp, tt, 4), F32),
                   jax.ShapeDtypeStruct((bn, nc, tt), F32)],
        compiler_params=_params("parallel"),
        name="mlstm_gates",
    )(gi, gf, padrow(ig_b), padrow(fg_b))


def _chunk_order(nchunks, reverse):
    return [0] + (list(range(nchunks - 1, 0, -1)) if reverse else list(range(1, nchunks)))


def _seg_valid(idx, off, tt):
    src = idx + off
    ok = (src >= 0) & (src < tt)
    return ok & ((idx < CHUNK) == (src < CHUNK))


def _mlstm_scan_kernel(q_ref, kt_ref, v_ref, o_ref, b_ref, m_ref, at_ref, cwq_ref, cbq_ref, cwk_ref, cbk_ref,
                       ng_ref, y_ref, qs_ref, ks_ref, hs_ref):
    tt = q_ref.shape[0]
    nchunks = tt // CHUNK
    pad = CONV_W // 2
    q = q_ref[...].astype(F32)
    kt = kt_ref[...].astype(F32)
    ridx = lax.broadcasted_iota(I32, (tt, 1), 0)
    lidx = lax.broadcasted_iota(I32, (1, tt), 1)
    qa = q * cwq_ref[pad:pad + 1, :] + cbq_ref[...]
    ka = kt * cwk_ref[:, pad:pad + 1] + cbk_ref[...]
    for w in range(CONV_W):
        off = w - pad
        if off == 0:
            continue
        qsh = pltpu.roll(q, (-off) % tt, axis=0)
        qa = qa + jnp.where(_seg_valid(ridx, off, tt), qsh, 0.0) * cwq_ref[w:w + 1, :]
        ksh = pltpu.roll(kt, (-off) % tt, axis=1)
        ka = ka + jnp.where(_seg_valid(lidx, off, tt), ksh, 0.0) * cwk_ref[:, w:w + 1]
    qs_ref[...] = (qa * _sigmoid(qa) * (A_DK ** -0.5)).astype(BF16)
    ks_ref[...] = (ka * _sigmoid(ka)).astype(BF16)

    ii = lax.broadcasted_iota(I32, (CHUNK, CHUNK), 0)
    jj = lax.broadcasted_iota(I32, (CHUNK, CHUNK), 1)
    ones_col = (lax.broadcasted_iota(I32, (CHUNK, A_DV), 1) == 0).astype(BF16)

    for hh in range(2):
        for d in range(2):
            mask = (jj >= ii) if d else (jj <= ii)
            last = 0 if d else CHUNK - 1
            state = jnp.zeros((A_DK, 2 * A_DV), F32)
            m = jnp.zeros((1, 1), F32)
            for c in _chunk_order(nchunks, bool(d)):
                r0 = c * CHUNK
                qc = qs_ref[r0:r0 + CHUNK, hh * A_DK:(hh + 1) * A_DK]
                ktc = ks_ref[hh * A_DK:(hh + 1) * A_DK, r0:r0 + CHUNK]
                vaug = jnp.concatenate([v_ref[r0:r0 + CHUNK, hh * A_DV:(hh + 1) * A_DV], ones_col], axis=1)
                col = slice(2 * d + hh, 2 * d + hh + 1)
                a_row = at_ref[col, r0:r0 + CHUNK]
                mcol = jnp.maximum(m, m_ref[r0:r0 + CHUNK, col])
                bcol = b_ref[r0:r0 + CHUNK, col]
                decay = jnp.where(mask, jnp.exp(a_row - mcol), 0.0)
                s = (_dot(qc, ktc) * decay).astype(BF16)
                w_inter = jnp.exp(m - mcol)
                nd = _dot(s, vaug) + w_inter * _dot(qc, state.astype(BF16))
                num = nd[:, :A_DV]
                den = nd[:, A_DV:A_DV + 1]
                hout = num / jnp.maximum(jnp.abs(den), jnp.exp(-(bcol + mcol)))
                dst = (slice(r0, r0 + CHUNK), slice(hh * A_DV, (hh + 1) * A_DV))
                if d:
                    hs_ref[dst] = hs_ref[dst] + hout
                else:
                    hs_ref[dst] = hout
                m_last = mcol[last:last + 1, :]
                wk = jnp.exp(a_row - m_last)
                state = jnp.exp(m - m_last) * state + _dot((ktc.astype(F32) * wk).astype(BF16), vaug)
                m = bcol[last:last + 1, :] + m_last
    for hh in range(2):
        sl = slice(hh * A_DV, (hh + 1) * A_DV)
        y = hs_ref[:, sl]
        y = y * lax.rsqrt(jnp.mean(y * y, axis=-1, keepdims=True) + EPS)
        y = y * ng_ref[:, sl] * _sigmoid(o_ref[:, sl].astype(F32))
        y_ref[:, sl] = y.astype(BF16)


def _mlstm_scan(q, kt, v, o, bcs, mloc, at, conv_w, conv_b, norm_g):
    bn, tt, nq = q.shape
    nv = v.shape[-1]
    nc = 2 * A_HEADS
    cwq = conv_w[:, :nq]
    cwk = conv_w[:, nq:].T
    cbq = conv_b[:nq].reshape(1, nq)
    cbk = conv_b[nq:].reshape(nq, 1)
    return pl.pallas_call(
        _mlstm_scan_kernel,
        grid=(bn, A_HEADS // 2),
        in_specs=[pl.BlockSpec((None, tt, 2 * A_DK), lambda b, h: (b, 0, h)),
                  pl.BlockSpec((None, 2 * A_DK, tt), lambda b, h: (b, h, 0)),
                  pl.BlockSpec((None, tt, 2 * A_DV), lambda b, h: (b, 0, h)),
                  pl.BlockSpec((None, tt, 2 * A_DV), lambda b, h: (b, 0, h)),
                  pl.BlockSpec((None, None, tt, 4), lambda b, h: (b, h, 0, 0)),
                  pl.BlockSpec((None, None, tt, 4), lambda b, h: (b, h, 0, 0)),
                  pl.BlockSpec((None, None, 4, tt), lambda b, h: (b, h, 0, 0)),
                  pl.BlockSpec((CONV_W, 2 * A_DK), lambda b, h: (0, h)),
                  pl.BlockSpec((1, 2 * A_DK), lambda b, h: (0, h)),
                  pl.BlockSpec((2 * A_DK, CONV_W), lambda b, h: (h, 0)),
                  pl.BlockSpec((2 * A_DK, 1), lambda b, h: (h, 0)),
                  pl.BlockSpec((1, 2 * A_DV), lambda b, h: (0, h))],
        out_specs=pl.BlockSpec((None, tt, 2 * A_DV), lambda b, h: (b, 0, h)),
        out_shape=jax.ShapeDtypeStruct((bn, tt, nv), BF16),
        scratch_shapes=[pltpu.VMEM((tt, 2 * A_DK), BF16), pltpu.VMEM((2 * A_DK, tt), BF16),
                        pltpu.VMEM((tt, 2 * A_DV), F32)],
        compiler_params=_params("parallel", "parallel"),
        name="mlstm_scan",
    )(q, kt, v, o, bcs, mloc, at.reshape(bn, A_HEADS // 2, 4, tt), cwq, cbq, cwk, cbk, norm_g.reshape(1, nv))


def _ret_scan_kernel(q_ref, kt_ref, v_ref, g_ref, dl0_ref, dl1_ref, gn_ref, y_ref, hs_ref):
    tt = q_ref.shape[0]
    nchunks = tt // CHUNK
    ii = lax.broadcasted_iota(I32, (CHUNK, CHUNK), 0)
    jj = lax.broadcasted_iota(I32, (CHUNK, CHUNK), 1)
    ci = lax.broadcasted_iota(I32, (CHUNK, 1), 0)
    rj = lax.broadcasted_iota(I32, (1, CHUNK), 1)
    for d, dl_ref in enumerate((dl0_ref, dl1_ref)):
        logg = _log_sigmoid(dl_ref[...])
        dist = (ii - jj) if d == 0 else (jj - ii)
        dmat = jnp.where(dist >= 0, jnp.exp(dist.astype(F32) * logg), 0.0)
        p_col = (ci if d == 0 else CHUNK - 1 - ci).astype(F32)
        p_row = (rj if d == 0 else CHUNK - 1 - rj).astype(F32)
        inter = jnp.exp((p_col + 1.0) * logg)
        kdec = jnp.exp((CHUNK - 1.0 - p_row) * logg)
        sdec = jnp.exp(CHUNK * logg)
        state = jnp.zeros((B_DK, B_DV), F32)
        for c in _chunk_order(nchunks, bool(d)):
            r0 = c * CHUNK
            qc = q_ref[r0:r0 + CHUNK, :]
            ktc = kt_ref[:, r0:r0 + CHUNK]
            vc = v_ref[r0:r0 + CHUNK, :]
            s = (_dot(qc, ktc) * dmat).astype(BF16)
            out = _dot(s, vc) + inter * _dot(qc, state.astype(BF16))
            if d:
                hs_ref[r0:r0 + CHUNK, :] = hs_ref[r0:r0 + CHUNK, :] + out
            else:
                hs_ref[r0:r0 + CHUNK, :] = out
            state = sdec * state + _dot((ktc.astype(F32) * kdec).astype(BF16), vc)
    y = hs_ref[...]
    mu = jnp.mean(y, axis=-1, keepdims=True)
    yc = y - mu
    var = jnp.mean(yc * yc, axis=-1, keepdims=True)
    g = g_ref[...].astype(F32)
    y_ref[...] = (yc * lax.rsqrt(var + EPS) * gn_ref[...] * (g * _sigmoid(g))).astype(BF16)


def _ret_scan(q, kt, v, g, decay_logit, gn_g):
    bn, tt, n = q.shape
    dl = decay_logit.astype(F32).reshape(2, B_HEADS, 1, 1)
    tok = pl.BlockSpec((None, tt, B_DK), lambda b, h: (b, 0, h))
    return pl.pallas_call(
        _ret_scan_kernel,
        grid=(bn, B_HEADS),
        in_specs=[tok, pl.BlockSpec((None, B_DK, tt), lambda b, h: (b, h, 0)), tok, tok,
                  pl.BlockSpec((None, None, 1, 1), lambda b, h: (0, h, 0, 0)),
                  pl.BlockSpec((None, None, 1, 1), lambda b, h: (1, h, 0, 0)),
                  pl.BlockSpec((1, B_DV), lambda b, h: (0, h))],
        out_specs=tok,
        out_shape=jax.ShapeDtypeStruct((bn, tt, n), BF16),
        scratch_shapes=[pltpu.VMEM((tt, B_DV), F32)],
        compiler_params=_params("parallel", "parallel"),
        name="ret_scan",
    )(q, kt, v, g, dl, dl, gn_g.reshape(1, n))


def _outproj_kernel(y_ref, x_ref, w_ref, ml_ref, mc_ref, g_ref, rt_ref, x1_ref, h2_ref, lg_ref):
    is_ctx = _is_ctx_rows(pl.program_id(1), x_ref.shape[0])
    out = _dot(y_ref[...], w_ref[...])
    x1 = x_ref[...] + _row_mod(ml_ref, mc_ref, 2, is_ctx) * out
    x1_ref[...] = x1
    h2 = _rms_mod(x1, g_ref[...], _row_mod(ml_ref, mc_ref, 3, is_ctx), _row_mod(ml_ref, mc_ref, 4, is_ctx))
    h2_ref[...] = h2.astype(BF16)
    lg_ref[...] = _nt_dot(rt_ref[...], h2, precision=HIGHEST)


def _outproj(y, xall, w_out, mod_l, mod_c, g, router):
    bn, tt, d = xall.shape
    tm = TOK_TILE
    ne = router.shape[1]
    return pl.pallas_call(
        _outproj_kernel,
        grid=(bn, tt // tm),
        in_specs=[_tok_spec(y.shape[-1], tm), _tok_spec(d, tm), _full_spec(w_out.shape)] + _mod_specs(d)
                 + [_full_spec((1, d)), _full_spec((ne, d))],
        out_specs=[_tok_spec(d, tm), _tok_spec(d, tm), pl.BlockSpec((None, ne, tm), lambda b, t: (b, 0, t))],
        out_shape=[jax.ShapeDtypeStruct((bn, tt, d), F32), jax.ShapeDtypeStruct((bn, tt, d), BF16),
                   jax.ShapeDtypeStruct((bn, ne, tt), F32)],
        compiler_params=_params("parallel", "parallel"),
        name="outproj_router",
    )(y, xall, w_out.astype(BF16), mod_l, mod_c, g.reshape(1, d), router.T.astype(F32))


def _lane_cumsum(x, upper):
    n = x.shape[1]
    parts = []
    carry = jnp.zeros((x.shape[0], 1), F32)
    for j in range(n // CHUNK):
        cs = _dot(x[:, j * CHUNK:(j + 1) * CHUNK].astype(BF16), upper) + carry
        parts.append(cs)
        carry = cs[:, CHUNK - 1:CHUNK]
    return parts[0] if len(parts) == 1 else jnp.concatenate(parts, axis=1)


def _route_kernel(lg_ref, pos_ref, gate_ref, post_ref, gatet_ref, *, segs):
    ne = lg_ref.shape[0]
    si = lax.broadcasted_iota(I32, (CHUNK, CHUNK), 0)
    ti = lax.broadcasted_iota(I32, (CHUNK, CHUNK), 1)
    upper = (si <= ti).astype(BF16)
    for (s0, n, cap, tag) in segs:
        lg = lg_ref[:, s0:s0 + n]
        e = jnp.exp(lg - jnp.max(lg, axis=0, keepdims=True))
        aff = e / jnp.sum(e, axis=0, keepdims=True)
        bits = pltpu.bitcast(aff, I32)
        tau = jnp.zeros((ne, 1), I32)
        for bit in range(30, -1, -1):
            cand = tau | (1 << bit)
            cnt = jnp.sum((bits >= cand).astype(F32), axis=1, keepdims=True)
            tau = jnp.where(cnt >= cap, cand, tau)
        gt = bits > tau
        eq = bits == tau
        need = cap - jnp.sum(gt.astype(F32), axis=1, keepdims=True)
        cum_eq = _lane_cumsum(eq.astype(F32), upper)
        sel = gt | (eq & (cum_eq <= need))
        pos = _lane_cumsum(sel.astype(F32), upper) - 1.0 + tag
        pos_ref[:, s0:s0 + n] = jnp.where(sel, pos, -1.0)
        gate_ref[:, s0:s0 + n] = aff
    pad = jnp.zeros((128 - ne, pos_ref.shape[1]), F32)
    post_ref[...] = jnp.concatenate([pos_ref[...], pad], axis=0).T[:, :ne]
    gatet_ref[...] = jnp.concatenate([gate_ref[...], pad], axis=0).T[:, :ne]


def _route(logits, segs):
    bn, ne, tt = logits.shape
    row = pl.BlockSpec((None, ne, tt), lambda b: (b, 0, 0))
    col = pl.BlockSpec((None, tt, ne), lambda b: (b, 0, 0))
    return pl.pallas_call(
        functools.partial(_route_kernel, segs=segs),
        grid=(bn,),
        in_specs=[row],
        out_specs=[row, row, col, col],
        out_shape=[jax.ShapeDtypeStruct((bn, ne, tt), F32), jax.ShapeDtypeStruct((bn, ne, tt), F32),
                   jax.ShapeDtypeStruct((bn, tt, ne), F32), jax.ShapeDtypeStruct((bn, tt, ne), F32)],
        compiler_params=_params("parallel"),
        name="route",
    )(logits)


EXP_GROUP = 4


def _gather_kernel(pos_ref, h_ref, xc_ref, xl_ref, *, cap_c, cap_l):
    tt = h_ref.shape[0]
    g = pl.program_id(1)

    def onehot(rows, lo, hi, cap, tag):
        parts = []
        for r in rows:
            p = pos_ref[r:r + 1, lo:hi] - tag
            c = lax.broadcasted_iota(I32, (cap, hi - lo), 0).astype(F32)
            parts.append(jnp.where(p == c, 1.0, 0.0).astype(BF16))
        return jnp.concatenate(parts, axis=0)

    pl_ = onehot(range(EXP_GROUP), CHUNK, tt, cap_l, 0.0)
    xl_ref[...] = _dot(pl_, h_ref[CHUNK:tt, :]).astype(BF16)
    pc = onehot(range(EXP_GROUP), 0, CHUNK, cap_c, float(CHUNK))
    xc_ref[...] = _dot(pc, h_ref[0:CHUNK, :]).astype(BF16)


def _gather(pos, h2, cap_c, cap_l):
    bn, ne, tt = pos.shape
    d = h2.shape[-1]
    ng = ne // EXP_GROUP
    pos4 = pos.reshape(bn, ng, EXP_GROUP, tt)
    return pl.pallas_call(
        functools.partial(_gather_kernel, cap_c=cap_c, cap_l=cap_l),
        grid=(bn, ng),
        in_specs=[pl.BlockSpec((None, None, EXP_GROUP, tt), lambda b, g: (b, g, 0, 0)),
                  pl.BlockSpec((None, tt, d), lambda b, g: (b, 0, 0))],
        out_specs=[pl.BlockSpec((None, EXP_GROUP * cap_c, d), lambda b, g: (b, g, 0)),
                   pl.BlockSpec((None, EXP_GROUP * cap_l, d), lambda b, g: (b, g, 0))],
        out_shape=[jax.ShapeDtypeStruct((bn, ne * cap_c, d), BF16), jax.ShapeDtypeStruct((bn, ne * cap_l, d), BF16)],
        compiler_params=_params("parallel", "arbitrary"),
        name="moe_gather",
    )(pos4, h2)


FFN_BATCH = 4


def _ffn_kernel(xc_ref, xl_ref, wg_ref, wu_ref, wd_ref, yc_ref, yl_ref):
    bb, cap_c, d = xc_ref.shape
    cap_l = xl_ref.shape[1]
    x = jnp.concatenate([xc_ref[...].reshape(bb * cap_c, d), xl_ref[...].reshape(bb * cap_l, d)], axis=0)
    g = _dot(x, wg_ref[...])
    u = _dot(x, wu_ref[...])
    hid = (g * _sigmoid(g) * u).astype(BF16)
    y = _dot(hid, wd_ref[...]).astype(BF16)
    yc_ref[...] = y[:bb * cap_c].reshape(bb, cap_c, d)
    yl_ref[...] = y[bb * cap_c:].reshape(bb, cap_l, d)


def _ffn(xc, xl, w_gate, w_up, w_down, layer, cap_c, cap_l):
    bn = xc.shape[0]
    _, ne, d, f = w_gate.shape
    bb = FFN_BATCH
    xc4 = xc.reshape(bn, ne, cap_c, d)
    xl4 = xl.reshape(bn, ne, cap_l, d)
    spec_c = pl.BlockSpec((bb, None, cap_c, d), lambda e, b: (b, e, 0, 0))
    spec_l = pl.BlockSpec((bb, None, cap_l, d), lambda e, b: (b, e, 0, 0))
    wspec = lambda s: pl.BlockSpec((None, None) + s, lambda e, b: (layer, e, 0, 0))
    yc, yl = pl.pallas_call(
        _ffn_kernel,
        grid=(ne, bn // bb),
        in_specs=[spec_c, spec_l, wspec((d, f)), wspec((d, f)), wspec((f, d))],
        out_specs=[spec_c, spec_l],
        out_shape=[jax.ShapeDtypeStruct(xc4.shape, BF16), jax.ShapeDtypeStruct(xl4.shape, BF16)],
        compiler_params=_params("parallel", "arbitrary"),
        name="moe_ffn",
    )(xc4, xl4, w_gate, w_up, w_down)
    return yc.reshape(bn, ne * cap_c, d), yl.reshape(bn, ne * cap_l, d)


def _scatter_kernel(post_ref, gatet_ref, yc_ref, yl_ref, x_ref, ml_ref, mc_ref, fg_ref, o_ref, *,
                    cap_c, cap_l, tile0, final):
    tm = x_ref.shape[0]
    ne = post_ref.shape[1]
    t = pl.program_id(1) + tile0

    def onehot(rows, cap, tag):
        nrows = rows.stop - rows.start
        c = lax.broadcasted_iota(I32, (nrows, cap), 1).astype(F32)
        parts = []
        for e in range(ne):
            p = post_ref[rows, e:e + 1] - tag
            parts.append(jnp.where(p == c, gatet_ref[rows, e:e + 1], 0.0).astype(BF16))
        return jnp.concatenate(parts, axis=1)

    is_ctx = _is_ctx_rows(t, tm)
    moe = _dot(onehot(slice(0, tm), cap_l, 0.0), yl_ref[...])
    x2 = x_ref[...] + _row_mod(ml_ref, mc_ref, 5, is_ctx) * moe
    if final:
        y = x2 * lax.rsqrt(jnp.mean(x2 * x2, axis=-1, keepdims=True) + EPS)
        o_ref[...] = y * fg_ref[...]
    else:
        o_ref[...] = x2

        @pl.when(t == 0)
        def _():
            rows = slice(0, CHUNK)
            moe_c = _dot(onehot(rows, cap_c, float(CHUNK)), yc_ref[...])
            o_ref[rows, :] = o_ref[rows, :] + mc_ref[5:6, :] * moe_c


def _scatter(post, gatet, yc, yl, x1, mod_l, mod_c, final_g, cap_c, cap_l, final):
    bn, tt, d = x1.shape
    ne = post.shape[-1]
    tm = CHUNK
    tile0 = 1 if final else 0
    ntiles = tt // tm - tile0
    tok = lambda cols: pl.BlockSpec((None, tm, cols), lambda b, t: (b, t + tile0, 0))
    whole = lambda a: pl.BlockSpec((None,) + a.shape[1:], lambda b, t: (b, 0, 0))
    return pl.pallas_call(
        functools.partial(_scatter_kernel, cap_c=cap_c, cap_l=cap_l, tile0=tile0, final=final),
        grid=(bn, ntiles),
        in_specs=[tok(ne), tok(ne), whole(yc), whole(yl), tok(d)] + _mod_specs(d) + [_full_spec((1, d))],
        out_specs=pl.BlockSpec((None, tm, d), lambda b, t: (b, t, 0)),
        out_shape=jax.ShapeDtypeStruct((bn, ntiles * tm, d), F32),
        compiler_params=_params("parallel", "arbitrary"),
        name="moe_scatter_final" if final else "moe_scatter",
    )(post, gatet, yc, yl, x1, mod_l, mod_c, final_g.reshape(1, d))


def kernel(x, c, ctx, c_ctx, ada_w, ada_b, norm_mix_g, norm_ffn_g, final_norm_g, mlstm_w_in, mlstm_conv_w,
           mlstm_conv_b, mlstm_igate_b, mlstm_fgate_b, mlstm_head_norm_g, mlstm_w_out, ret_w_in, ret_decay_logit,
           ret_group_norm_g, ret_w_out, moe_router, moe_w_gate, moe_w_up, moe_w_down):
    bn, seq, d = x.shape
    depth = ada_w.shape[0]
    assert ctx.shape[1] == CHUNK and seq % CHUNK == 0 and (seq + CHUNK) % TOK_TILE == 0
    tt = CHUNK + seq
    cap_c = CAPACITY_FACTOR * CHUNK // N_EXPERTS
    cap_l = CAPACITY_FACTOR * seq // N_EXPERTS
    assert cap_l == CHUNK
    segs = ((0, CHUNK, cap_c, float(CHUNK)), (CHUNK, seq, cap_l, 0.0))

    rows = -(-(bn + 1) // 8) * 8
    cvec = jnp.concatenate([c, c_ctx[None, :], jnp.zeros((rows - bn - 1, d), F32)], axis=0)
    mod = _adaln(cvec, ada_w, ada_b)
    xall = jnp.concatenate([ctx, x], axis=1)
    wg16 = moe_w_gate.astype(BF16)
    wu16 = moe_w_up.astype(BF16)
    wd16 = moe_w_down.astype(BF16)

    for i in range(depth):
        last = i == depth - 1
        mod_l = mod[i, :bn].reshape(bn, N_MOD, d)
        mod_c = mod[i, bn].reshape(N_MOD, d)
        j = i // 2
        if i % 2 == 0:
            q, kt, v, o, gi, gf = _inproj_a(xall, norm_mix_g[i], mod_l, mod_c, mlstm_w_in[j])
            bcs, mloc, at = _mlstm_gates(gi, gf, mlstm_igate_b[j], mlstm_fgate_b[j])
            y = _mlstm_scan(q, kt, v, o, bcs, mloc, at, mlstm_conv_w[j], mlstm_conv_b[j], mlstm_head_norm_g[j])
            w_out = mlstm_w_out[j]
        else:
            q, kt, v, g = _inproj_b(xall, norm_mix_g[i], mod_l, mod_c, ret_w_in[j])
            y = _ret_scan(q, kt, v, g, ret_decay_logit[j], ret_group_norm_g[j])
            w_out = ret_w_out[j]
        x1, h2, logits = _outproj(y, xall, w_out, mod_l, mod_c, norm_ffn_g[i], moe_router[i])
        pos, _, post, gatet = _route(logits, segs)
        xc, xl = _gather(pos, h2, cap_c, cap_l)
        yc, yl = _ffn(xc, xl, wg16, wu16, wd16, i, cap_c, cap_l)
        xall = _scatter(post, gatet, yc, yl, x1, mod_l, mod_c, final_norm_g, cap_c, cap_l, last)
    return xall
```

```python
import functools

import jax
import jax.numpy as jnp
from jax import lax
from jax.experimental import pallas as pl
from jax.experimental.pallas import tpu as pltpu

F32 = jnp.float32
BF16 = jnp.bfloat16
I32 = jnp.int32
HIGHEST = lax.Precision.HIGHEST

EPS = 1e-6
N_MOD = 6
CHUNK = 256
TOK_TILE = 768
A_HEADS, A_DK, A_DV = 8, 64, 128
B_HEADS, B_DK, B_DV = 8, 128, 128
CONV_W = 5
CAPACITY_FACTOR = 2
GRID_W = 64
ROPE_BASE = 10000.0
LANES = 128
VMEM_LIMIT = 56 * 1024 * 1024


def _params(*sem):
    return pltpu.CompilerParams(dimension_semantics=sem, vmem_limit_bytes=VMEM_LIMIT)


def _sigmoid(x):
    return 1.0 / (1.0 + jnp.exp(-x))


def _log_sigmoid(x):
    return jnp.minimum(x, 0.0) - jnp.log1p(jnp.exp(-jnp.abs(x)))


def _nt_dot(a, b, **kw):
    return lax.dot_general(a, b, (((1,), (1,)), ((), ())), preferred_element_type=F32, **kw)


def _dot(a, b, **kw):
    return jnp.dot(a, b, preferred_element_type=F32, **kw)


def _rms_mod(x, g, shift, scale):
    y = x * lax.rsqrt(jnp.mean(x * x, axis=-1, keepdims=True) + EPS)
    return (y * g) * (1.0 + scale) + shift


def _tile_mod(mod_l_ref, mod_c_ref, tile, ntiles, rows, sub):
    if (sub + 1) * CHUNK < rows:
        return mod_l_ref[...]
    return jnp.where(tile == ntiles - 1, mod_c_ref[...], mod_l_ref[...])


def _adaln_kernel(c_ref, w_ref, b_ref, o_ref):
    c = c_ref[...]
    s = c * _sigmoid(c)
    o_ref[...] = _dot(s, w_ref[...], precision=HIGHEST) + b_ref[...]


def _adaln(cvec, ada_w, ada_b):
    depth, d, n = ada_w.shape
    rows = cvec.shape[0]
    tn = 1536
    return pl.pallas_call(
        _adaln_kernel,
        grid=(depth, n // tn),
        in_specs=[pl.BlockSpec((rows, d), lambda l, j: (0, 0)),
                  pl.BlockSpec((None, d, tn), lambda l, j: (l, 0, j)),
                  pl.BlockSpec((None, 1, tn), lambda l, j: (l, 0, j))],
        out_specs=pl.BlockSpec((None, rows, tn), lambda l, j: (l, 0, j)),
        out_shape=jax.ShapeDtypeStruct((depth, rows, n), F32),
        compiler_params=_params("parallel", "parallel"),
        name="adaln",
    )(cvec, ada_w, ada_b.reshape(depth, 1, n))


def _inproj_a_kernel(x_ref, g_ref, ml_ref, mc_ref, wq_ref, wkt_ref, wv_ref, wo_ref, wgi_ref, wgf_ref,
                     q_ref, kt_ref, v_ref, o_ref, gi_ref, gf_ref):
    for s in range(x_ref.shape[0] // CHUNK):
        rows = slice(s * CHUNK, (s + 1) * CHUNK)
        mod = _tile_mod(ml_ref, mc_ref, pl.program_id(1), pl.num_programs(1), x_ref.shape[0], s)
        h = _rms_mod(x_ref[rows, :], g_ref[...], mod[0:1, :], mod[1:2, :]).astype(BF16)
        q_ref[rows, :] = _dot(h, wq_ref[...]).astype(BF16)
        kt_ref[:, rows] = _nt_dot(wkt_ref[...], h).astype(BF16)
        v_ref[rows, :] = _dot(h, wv_ref[...]).astype(BF16)
        o_ref[rows, :] = _dot(h, wo_ref[...]).astype(BF16)
        gi_ref[rows, :] = _dot(h, wgi_ref[...])
        gf_ref[rows, :] = _dot(h, wgf_ref[...])


def _inproj_b_kernel(x_ref, g_ref, ml_ref, mc_ref, cs_ref, sn_ref, cst_ref, snt_ref,
                     wq_ref, wkt_ref, wv_ref, wg_ref, q_ref, kt_ref, v_ref, gate_ref):
    lane = lax.broadcasted_iota(I32, (CHUNK, B_DK), 1)
    first = (lane % 64) < 32
    for s in range(x_ref.shape[0] // CHUNK):
        rows = slice(s * CHUNK, (s + 1) * CHUNK)
        mod = _tile_mod(ml_ref, mc_ref, pl.program_id(1), pl.num_programs(1), x_ref.shape[0], s)
        h = _rms_mod(x_ref[rows, :], g_ref[...], mod[0:1, :], mod[1:2, :]).astype(BF16)
        v_ref[rows, :] = _dot(h, wv_ref[...]).astype(BF16)
        gate_ref[rows, :] = _dot(h, wg_ref[...]).astype(BF16)
        cs, sn = cs_ref[rows, :], sn_ref[rows, :]
        q = _dot(h, wq_ref[...])
        for hd in range(B_HEADS):
            qh = q[:, hd * B_DK:(hd + 1) * B_DK]
            partner = jnp.where(first, pltpu.roll(qh, B_DK - 32, axis=1), pltpu.roll(qh, 32, axis=1))
            q_ref[rows, hd * B_DK:(hd + 1) * B_DK] = (qh * cs + partner * sn).astype(BF16)
        kt = _nt_dot(wkt_ref[...], h) * (B_DK ** -0.5)
        cst, snt = cst_ref[:, rows], snt_ref[:, rows]
        for hd in range(B_HEADS):
            for half in range(2):
                base = hd * B_DK + half * 64
                x1 = kt[base:base + 32, :]
                x2 = kt[base + 32:base + 64, :]
                c_ = cst[half * 32:(half + 1) * 32, :]
                s_ = snt[half * 32:(half + 1) * 32, :]
                kt_ref[base:base + 32, rows] = (x1 * c_ - x2 * s_).astype(BF16)
                kt_ref[base + 32:base + 64, rows] = (x1 * s_ + x2 * c_).astype(BF16)


def _tok_spec(cols, tm):
    return pl.BlockSpec((None, tm, cols), lambda b, t: (b, t, 0))


def _full_spec(shape):
    nd = len(shape)
    return pl.BlockSpec(shape, lambda b, t: (0,) * nd, pipeline_mode=pl.Buffered(1))


def _mod_specs(d):
    return [pl.BlockSpec((None, N_MOD, d), lambda b, t: (b, 0, 0)), _full_spec((N_MOD, d))]


def _inproj_a(xall, g, mod_l, mod_c, w_in):
    bn, tt, d = xall.shape
    tm = TOK_TILE
    nq, nv = A_HEADS * A_DK, A_HEADS * A_DV
    wq = w_in[:, :nq].astype(BF16)
    wkt = w_in[:, nq:2 * nq].T.astype(BF16)
    wv = w_in[:, 2 * nq:2 * nq + nv].astype(BF16)
    wo = w_in[:, 2 * nq + nv:2 * nq + 2 * nv].astype(BF16)
    wg = w_in[:, 2 * nq + 2 * nv:]
    pad = jnp.zeros((d, 128 - 2 * A_HEADS), F32)
    perm = jnp.array(_GATE_PERM)
    wgi = jnp.concatenate([wg[:, :2 * A_HEADS][:, perm], pad], axis=1).astype(BF16)
    wgf = jnp.concatenate([wg[:, 2 * A_HEADS:][:, perm], pad], axis=1).astype(BF16)
    ws = [wq, wkt, wv, wo, wgi, wgf]
    return pl.pallas_call(
        _inproj_a_kernel,
        grid=(bn, tt // tm),
        in_specs=[_tok_spec(d, tm), _full_spec((1, d))] + _mod_specs(d) + [_full_spec(w.shape) for w in ws],
        out_specs=[_tok_spec(nq, tm), pl.BlockSpec((None, nq, tm), lambda b, t: (b, 0, t)),
                   _tok_spec(nv, tm), _tok_spec(nv, tm), _tok_spec(128, tm), _tok_spec(128, tm)],
        out_shape=[jax.ShapeDtypeStruct((bn, tt, nq), BF16), jax.ShapeDtypeStruct((bn, nq, tt), BF16),
                   jax.ShapeDtypeStruct((bn, tt, nv), BF16), jax.ShapeDtypeStruct((bn, tt, nv), BF16),
                   jax.ShapeDtypeStruct((bn, tt, 128), F32), jax.ShapeDtypeStruct((bn, tt, 128), F32)],
        compiler_params=_params("parallel", "parallel"),
        name="inproj_mlstm",
    )(xall, g.reshape(1, d), mod_l, mod_c, *ws)


def _rope_tables(tt):
    seq = tt - CHUNK
    rows = seq // GRID_W
    r = jnp.broadcast_to(jnp.arange(rows, dtype=F32)[:, None], (rows, GRID_W)).reshape(-1)
    col = jnp.broadcast_to(jnp.arange(GRID_W, dtype=F32)[None, :], (rows, GRID_W)).reshape(-1)
    nf = B_DK // 4
    inv = ROPE_BASE ** (-jnp.arange(nf, dtype=F32) / nf)
    ang = jnp.concatenate([r[:, None] * inv, col[:, None] * inv], axis=1)
    cos = jnp.concatenate([jnp.cos(ang), jnp.ones((CHUNK, 2 * nf), F32)], axis=0)
    sin = jnp.concatenate([jnp.sin(ang), jnp.zeros((CHUNK, 2 * nf), F32)], axis=0)
    cr, cc, sr, sc = cos[:, :nf], cos[:, nf:], sin[:, :nf], sin[:, nf:]
    cs = jnp.concatenate([cr, cr, cc, cc], axis=1)
    sn = jnp.concatenate([-sr, sr, -sc, sc], axis=1)
    return cs, sn, cos.T, sin.T


def _inproj_b(xall, g, mod_l, mod_c, w_in):
    bn, tt, d = xall.shape
    tm = TOK_TILE
    n = B_HEADS * B_DK
    wq = w_in[:, :n].astype(BF16)
    wkt = w_in[:, n:2 * n].T.astype(BF16)
    wv = w_in[:, 2 * n:3 * n].astype(BF16)
    wg = w_in[:, 3 * n:].astype(BF16)
    cs, sn, cst, snt = _rope_tables(tt)
    ws = [wq, wkt, wv, wg]
    tab_specs = [pl.BlockSpec((tm, B_DK), lambda b, t: (t, 0)), pl.BlockSpec((tm, B_DK), lambda b, t: (t, 0)),
                 pl.BlockSpec((64, tm), lambda b, t: (0, t)), pl.BlockSpec((64, tm), lambda b, t: (0, t))]
    return pl.pallas_call(
        _inproj_b_kernel,
        grid=(bn, tt // tm),
        in_specs=[_tok_spec(d, tm), _full_spec((1, d))] + _mod_specs(d) + tab_specs
                 + [_full_spec(w.shape) for w in ws],
        out_specs=[_tok_spec(n, tm), pl.BlockSpec((None, n, tm), lambda b, t: (b, 0, t)),
                   _tok_spec(n, tm), _tok_spec(n, tm)],
        out_shape=[jax.ShapeDtypeStruct((bn, tt, n), BF16), jax.ShapeDtypeStruct((bn, n, tt), BF16),
                   jax.ShapeDtypeStruct((bn, tt, n), BF16), jax.ShapeDtypeStruct((bn, tt, n), BF16)],
        compiler_params=_params("parallel", "parallel"),
        name="inproj_ret",
    )(xall, g.reshape(1, d), mod_l, mod_c, cs, sn, cst, snt, *ws)


_GATE_PERM = [d * A_HEADS + 2 * hp + hh for hp in range(A_HEADS // 2) for d in range(2) for hh in range(2)]


def _chunk_scan(x, pos, op, reverse):
    n = x.shape[0]
    s = 1
    while s < CHUNK:
        if reverse:
            sh = pltpu.roll(x, n - s, axis=0)
            ok = pos < CHUNK - s
        else:
            sh = pltpu.roll(x, s, axis=0)
            ok = pos >= s
        x = jnp.where(ok, op(x, sh), x)
        s *= 2
    return x


def _mlstm_gates_kernel(gi_ref, gf_ref, bi_ref, bf_ref, b_ref, m_ref, at_ref):
    ig = gi_ref[...] + bi_ref[...]
    lf = _log_sigmoid(gf_ref[...] + bf_ref[...])
    tt = ig.shape[0]
    pos = lax.broadcasted_iota(I32, (tt, 1), 0) % CHUNK
    fwd = (lax.broadcasted_iota(I32, (1, ig.shape[1]), 1) // 2) % 2 == 0
    b = jnp.where(fwd, _chunk_scan(lf, pos, jnp.add, False), _chunk_scan(lf, pos, jnp.add, True))
    a = ig - b
    mloc = jnp.where(fwd, _chunk_scan(a, pos, jnp.maximum, False), _chunk_scan(a, pos, jnp.maximum, True))
    for hp in range(A_HEADS // 2):
        b_ref[hp] = b[:, 4 * hp:4 * hp + 4]
        m_ref[hp] = mloc[:, 4 * hp:4 * hp + 4]
    at_ref[...] = a.T[:2 * A_HEADS, :]


def _mlstm_gates(gi, gf, ig_b, fg_b):
    bn, tt, w = gi.shape
    nc = 2 * A_HEADS
    perm = jnp.array(_GATE_PERM)
    padrow = lambda v: jnp.concatenate([v.reshape(nc)[perm].reshape(1, nc).astype(F32),
                                        jnp.zeros((1, w - nc), F32)], axis=1)
    nhp = A_HEADS // 2
    spec = pl.BlockSpec((None, tt, w), lambda b: (b, 0, 0))
    row = pl.BlockSpec((1, w), lambda b: (0, 0))
    return pl.pallas_call(
        _mlstm_gates_kernel,
        grid=(bn,),
        in_specs=[spec, spec, row, row],
        out_specs=[pl.BlockSpec((None, nhp, tt, 4), lambda b: (b, 0, 0, 0)),
                   pl.BlockSpec((None, nhp, tt, 4), lambda b: (b, 0, 0, 0)),
                   pl.BlockSpec((None, nc, tt), lambda b: (b, 0, 0))],
        out_shape=[jax.ShapeDtypeStruct((bn, nhp, tt, 4), F32), jax.ShapeDtypeStruct((bn, nhp, tt, 4), F32),
                   jax.ShapeDtypeStruct((bn, nc, tt), F32)],
        compiler_params=_params("parallel"),
        name="mlstm_gates",
    )(gi, gf, padrow(ig_b), padrow(fg_b))


def _chunk_order(nchunks, reverse):
    lat = list(range(nchunks - 1))
    return [nchunks - 1] + (lat[::-1] if reverse else lat)


def _seg_valid(idx, off, tt):
    src = idx + off
    ok = (src >= 0) & (src < tt)
    return ok & ((idx < tt - CHUNK) == (src < tt - CHUNK))


def _mlstm_scan_kernel(q_ref, kt_ref, v_ref, o_ref, b_ref, m_ref, at_ref, cwq_ref, cbq_ref, cwk_ref, cbk_ref,
                       ng_ref, y_ref, qs_ref, ks_ref, hs_ref):
    tt = q_ref.shape[0]
    nchunks = tt // CHUNK
    pad = CONV_W // 2
    q = q_ref[...].astype(F32)
    kt = kt_ref[...].astype(F32)
    ridx = lax.broadcasted_iota(I32, (tt, 1), 0)
    lidx = lax.broadcasted_iota(I32, (1, tt), 1)
    qa = q * cwq_ref[pad:pad + 1, :] + cbq_ref[...]
    ka = kt * cwk_ref[:, pad:pad + 1] + cbk_ref[...]
    for w in range(CONV_W):
        off = w - pad
        if off == 0:
            continue
        qsh = pltpu.roll(q, (-off) % tt, axis=0)
        qa = qa + jnp.where(_seg_valid(ridx, off, tt), qsh, 0.0) * cwq_ref[w:w + 1, :]
        ksh = pltpu.roll(kt, (-off) % tt, axis=1)
        ka = ka + jnp.where(_seg_valid(lidx, off, tt), ksh, 0.0) * cwk_ref[:, w:w + 1]
    qs_ref[...] = (qa * _sigmoid(qa) * (A_DK ** -0.5)).astype(BF16)
    ks_ref[...] = (ka * _sigmoid(ka)).astype(BF16)

    ii = lax.broadcasted_iota(I32, (CHUNK, CHUNK), 0)
    jj = lax.broadcasted_iota(I32, (CHUNK, CHUNK), 1)
    ones_col = (lax.broadcasted_iota(I32, (CHUNK, A_DV), 1) == 0).astype(BF16)

    for hh in range(2):
        for d in range(2):
            mask = (jj >= ii) if d else (jj <= ii)
            last = 0 if d else CHUNK - 1
            state = jnp.zeros((A_DK, 2 * A_DV), F32)
            m = jnp.zeros((1, 1), F32)
            for c in _chunk_order(nchunks, bool(d)):
                r0 = c * CHUNK
                qc = qs_ref[r0:r0 + CHUNK, hh * A_DK:(hh + 1) * A_DK]
                ktc = ks_ref[hh * A_DK:(hh + 1) * A_DK, r0:r0 + CHUNK]
                vaug = jnp.concatenate([v_ref[r0:r0 + CHUNK, hh * A_DV:(hh + 1) * A_DV], ones_col], axis=1)
                col = slice(2 * d + hh, 2 * d + hh + 1)
                a_row = at_ref[col, r0:r0 + CHUNK]
                mcol = jnp.maximum(m, m_ref[r0:r0 + CHUNK, col])
                bcol = b_ref[r0:r0 + CHUNK, col]
                decay = jnp.where(mask, jnp.exp(a_row - mcol), 0.0)
                s = (_dot(qc, ktc) * decay).astype(BF16)
                w_inter = jnp.exp(m - mcol)
                nd = _dot(s, vaug) + w_inter * _dot(qc, state.astype(BF16))
                num = nd[:, :A_DV]
                den = nd[:, A_DV:A_DV + 1]
                hout = num / jnp.maximum(jnp.abs(den), jnp.exp(-(bcol + mcol)))
                dst = (slice(r0, r0 + CHUNK), slice(hh * A_DV, (hh + 1) * A_DV))
                if d:
                    hs_ref[dst] = hs_ref[dst] + hout
                else:
                    hs_ref[dst] = hout
                m_last = mcol[last:last + 1, :]
                wk = jnp.exp(a_row - m_last)
                state = jnp.exp(m - m_last) * state + _dot((ktc.astype(F32) * wk).astype(BF16), vaug)
                m = bcol[last:last + 1, :] + m_last
    for hh in range(2):
        sl = slice(hh * A_DV, (hh + 1) * A_DV)
        y = hs_ref[:, sl]
        y = y * lax.rsqrt(jnp.mean(y * y, axis=-1, keepdims=True) + EPS)
        y = y * ng_ref[:, sl] * _sigmoid(o_ref[:, sl].astype(F32))
        y_ref[:, sl] = y.astype(BF16)


def _mlstm_scan(q, kt, v, o, bcs, mloc, at, conv_w, conv_b, norm_g):
    bn, tt, nq = q.shape
    nv = v.shape[-1]
    nc = 2 * A_HEADS
    cwq = conv_w[:, :nq]
    cwk = conv_w[:, nq:].T
    cbq = conv_b[:nq].reshape(1, nq)
    cbk = conv_b[nq:].reshape(nq, 1)
    return pl.pallas_call(
        _mlstm_scan_kernel,
        grid=(bn, A_HEADS // 2),
        in_specs=[pl.BlockSpec((None, tt, 2 * A_DK), lambda b, h: (b, 0, h)),
                  pl.BlockSpec((None, 2 * A_DK, tt), lambda b, h: (b, h, 0)),
                  pl.BlockSpec((None, tt, 2 * A_DV), lambda b, h: (b, 0, h)),
                  pl.BlockSpec((None, tt, 2 * A_DV), lambda b, h: (b, 0, h)),
                  pl.BlockSpec((None, None, tt, 4), lambda b, h: (b, h, 0, 0)),
                  pl.BlockSpec((None, None, tt, 4), lambda b, h: (b, h, 0, 0)),
                  pl.BlockSpec((None, None, 4, tt), lambda b, h: (b, h, 0, 0)),
                  pl.BlockSpec((CONV_W, 2 * A_DK), lambda b, h: (0, h)),
                  pl.BlockSpec((1, 2 * A_DK), lambda b, h: (0, h)),
                  pl.BlockSpec((2 * A_DK, CONV_W), lambda b, h: (h, 0)),
                  pl.BlockSpec((2 * A_DK, 1), lambda b, h: (h, 0)),
                  pl.BlockSpec((1, 2 * A_DV), lambda b, h: (0, h))],
        out_specs=pl.BlockSpec((None, tt, 2 * A_DV), lambda b, h: (b, 0, h)),
        out_shape=jax.ShapeDtypeStruct((bn, tt, nv), BF16),
        scratch_shapes=[pltpu.VMEM((tt, 2 * A_DK), BF16), pltpu.VMEM((2 * A_DK, tt), BF16),
                        pltpu.VMEM((tt, 2 * A_DV), F32)],
        compiler_params=_params("parallel", "parallel"),
        name="mlstm_scan",
    )(q, kt, v, o, bcs, mloc, at.reshape(bn, A_HEADS // 2, 4, tt), cwq, cbq, cwk, cbk, norm_g.reshape(1, nv))


def _ret_scan_kernel(q_ref, kt_ref, v_ref, g_ref, dl0_ref, dl1_ref, gn_ref, y_ref, hs_ref):
    tt = q_ref.shape[0]
    nchunks = tt // CHUNK
    ii = lax.broadcasted_iota(I32, (CHUNK, CHUNK), 0)
    jj = lax.broadcasted_iota(I32, (CHUNK, CHUNK), 1)
    ci = lax.broadcasted_iota(I32, (CHUNK, 1), 0)
    rj = lax.broadcasted_iota(I32, (1, CHUNK), 1)
    for d, dl_ref in enumerate((dl0_ref, dl1_ref)):
        logg = _log_sigmoid(dl_ref[...])
        dist = (ii - jj) if d == 0 else (jj - ii)
        dmat = jnp.where(dist >= 0, jnp.exp(dist.astype(F32) * logg), 0.0)
        p_col = (ci if d == 0 else CHUNK - 1 - ci).astype(F32)
        p_row = (rj if d == 0 else CHUNK - 1 - rj).astype(F32)
        inter = jnp.exp((p_col + 1.0) * logg)
        kdec = jnp.exp((CHUNK - 1.0 - p_row) * logg)
        sdec = jnp.exp(CHUNK * logg)
        state = jnp.zeros((B_DK, B_DV), F32)
        for c in _chunk_order(nchunks, bool(d)):
            r0 = c * CHUNK
            qc = q_ref[r0:r0 + CHUNK, :]
            ktc = kt_ref[:, r0:r0 + CHUNK]
            vc = v_ref[r0:r0 + CHUNK, :]
            s = (_dot(qc, ktc) * dmat).astype(BF16)
            out = _dot(s, vc) + inter * _dot(qc, state.astype(BF16))
            if d:
                hs_ref[r0:r0 + CHUNK, :] = hs_ref[r0:r0 + CHUNK, :] + out
            else:
                hs_ref[r0:r0 + CHUNK, :] = out
            state = sdec * state + _dot((ktc.astype(F32) * kdec).astype(BF16), vc)
    y = hs_ref[...]
    mu = jnp.mean(y, axis=-1, keepdims=True)
    yc = y - mu
    var = jnp.mean(yc * yc, axis=-1, keepdims=True)
    g = g_ref[...].astype(F32)
    y_ref[...] = (yc * lax.rsqrt(var + EPS) * gn_ref[...] * (g * _sigmoid(g))).astype(BF16)


def _ret_scan(q, kt, v, g, decay_logit, gn_g):
    bn, tt, n = q.shape
    dl = decay_logit.astype(F32).reshape(2, B_HEADS, 1, 1)
    tok = pl.BlockSpec((None, tt, B_DK), lambda b, h: (b, 0, h))
    return pl.pallas_call(
        _ret_scan_kernel,
        grid=(bn, B_HEADS),
        in_specs=[tok, pl.BlockSpec((None, B_DK, tt), lambda b, h: (b, h, 0)), tok, tok,
                  pl.BlockSpec((None, None, 1, 1), lambda b, h: (0, h, 0, 0)),
                  pl.BlockSpec((None, None, 1, 1), lambda b, h: (1, h, 0, 0)),
                  pl.BlockSpec((1, B_DV), lambda b, h: (0, h))],
        out_specs=tok,
        out_shape=jax.ShapeDtypeStruct((bn, tt, n), BF16),
        scratch_shapes=[pltpu.VMEM((tt, B_DV), F32)],
        compiler_params=_params("parallel", "parallel"),
        name="ret_scan",
    )(q, kt, v, g, dl, dl, gn_g.reshape(1, n))


def _outproj_kernel(y_ref, x_ref, w_ref, ml_ref, mc_ref, g_ref, r2_ref, x1_ref, h2_ref, lg_ref):
    for s in range(x_ref.shape[0] // CHUNK):
        rows = slice(s * CHUNK, (s + 1) * CHUNK)
        mod = _tile_mod(ml_ref, mc_ref, pl.program_id(1), pl.num_programs(1), x_ref.shape[0], s)
        out = _dot(y_ref[rows, :], w_ref[...])
        x1 = x_ref[rows, :] + mod[2:3, :] * out
        x1_ref[rows, :] = x1
        h2 = _rms_mod(x1, g_ref[...], mod[3:4, :], mod[4:5, :])
        hi = h2.astype(BF16)
        h2_ref[rows, :] = hi
        lo = (h2 - hi.astype(F32)).astype(BF16)
        a = _dot(hi, r2_ref[...])
        lg_ref[rows, :] = (a[:, :LANES] + a[:, LANES:]) + _dot(lo, r2_ref[:, :LANES])


def _outproj(y, xall, w_out, mod_l, mod_c, g, router):
    bn, tt, d = xall.shape
    tm = TOK_TILE
    ne = router.shape[1]
    r = jnp.concatenate([router.astype(F32), jnp.zeros((d, LANES - ne), F32)], axis=1)
    rhi = r.astype(BF16)
    r2 = jnp.concatenate([rhi, (r - rhi.astype(F32)).astype(BF16)], axis=1)
    return pl.pallas_call(
        _outproj_kernel,
        grid=(bn, tt // tm),
        in_specs=[_tok_spec(y.shape[-1], tm), _tok_spec(d, tm), _full_spec(w_out.shape)] + _mod_specs(d)
                 + [_full_spec((1, d)), _full_spec((d, 2 * LANES))],
        out_specs=[_tok_spec(d, tm), _tok_spec(d, tm), _tok_spec(LANES, tm)],
        out_shape=[jax.ShapeDtypeStruct((bn, tt, d), F32), jax.ShapeDtypeStruct((bn, tt, d), BF16),
                   jax.ShapeDtypeStruct((bn, tt, LANES), F32)],
        compiler_params=_params("parallel", "parallel"),
        name="outproj_router",
    )(y, xall, w_out.astype(BF16), mod_l, mod_c, g.reshape(1, d), r2)


def _lane_cumsum(x, upper):
    n = x.shape[1]
    parts = []
    carry = jnp.zeros((x.shape[0], 1), F32)
    for j in range(n // CHUNK):
        cs = _dot(x[:, j * CHUNK:(j + 1) * CHUNK].astype(BF16), upper) + carry
        parts.append(cs)
        carry = cs[:, CHUNK - 1:CHUNK]
    return parts[0] if len(parts) == 1 else jnp.concatenate(parts, axis=1)


def _route_kernel(lg_ref, pos_ref, post_ref, gatet_ref, gate_ref, *, segs):
    ne = pos_ref.shape[0]
    si = lax.broadcasted_iota(I32, (CHUNK, CHUNK), 0)
    ti = lax.broadcasted_iota(I32, (CHUNK, CHUNK), 1)
    upper = (si <= ti).astype(BF16)
    lgt = lg_ref[...].T[:ne, :]
    for (s0, n, cap, tag) in segs:
        lg = lgt[:, s0:s0 + n]
        e = jnp.exp(lg - jnp.max(lg, axis=0, keepdims=True))
        aff = e / jnp.sum(e, axis=0, keepdims=True)
        bits = pltpu.bitcast(aff, I32)
        tau = jnp.zeros((ne, 1), I32)
        for bit in range(30, -1, -1):
            cand = tau | (1 << bit)
            cnt = jnp.sum((bits >= cand).astype(F32), axis=1, keepdims=True)
            tau = jnp.where(cnt >= cap, cand, tau)
        gt = bits > tau
        eq = bits == tau
        need = cap - jnp.sum(gt.astype(F32), axis=1, keepdims=True)
        cum_eq = _lane_cumsum(eq.astype(F32), upper)
        sel = gt | (eq & (cum_eq <= need))
        pos = _lane_cumsum(sel.astype(F32), upper) - 1.0 + tag
        pos_ref[:, s0:s0 + n] = jnp.where(sel, pos, -1.0)
        gate_ref[:, s0:s0 + n] = aff
    pad = jnp.zeros((LANES - ne, pos_ref.shape[1]), F32)
    post_ref[...] = jnp.concatenate([pos_ref[...], pad], axis=0).T[:, :ne]
    gatet_ref[...] = jnp.concatenate([gate_ref[...], pad], axis=0).T[:, :ne]


def _route(logits, ne, segs):
    bn, tt, w = logits.shape
    row = pl.BlockSpec((None, ne, tt), lambda b: (b, 0, 0))
    col = pl.BlockSpec((None, tt, ne), lambda b: (b, 0, 0))
    return pl.pallas_call(
        functools.partial(_route_kernel, segs=segs),
        grid=(bn,),
        in_specs=[pl.BlockSpec((None, tt, w), lambda b: (b, 0, 0))],
        out_specs=[row, col, col],
        out_shape=[jax.ShapeDtypeStruct((bn, ne, tt), F32),
                   jax.ShapeDtypeStruct((bn, tt, ne), F32), jax.ShapeDtypeStruct((bn, tt, ne), F32)],
        scratch_shapes=[pltpu.VMEM((ne, tt), F32)],
        compiler_params=_params("parallel"),
        name="route",
    )(logits)


EXP_GROUP = 4


def _onehot_rows(pos_ref, lo, hi, cap, tag):
    parts = []
    slot = lax.broadcasted_iota(I32, (cap, hi - lo), 0).astype(F32) + tag
    for r in range(EXP_GROUP):
        parts.append(jnp.where(pos_ref[r:r + 1, lo:hi] == slot, 1.0, 0.0).astype(BF16))
    return jnp.concatenate(parts, axis=0)


def _gather_kernel(pos_ref, h_ref, xl_ref, *maybe_xc_ref, cap_c, cap_l):
    tt = h_ref.shape[0]
    seq = tt - CHUNK
    xl_ref[...] = _dot(_onehot_rows(pos_ref, 0, seq, cap_l, 0.0), h_ref[0:seq, :]).astype(BF16)
    for xc_ref in maybe_xc_ref:
        xc_ref[...] = _dot(_onehot_rows(pos_ref, seq, tt, cap_c, float(cap_l)), h_ref[seq:tt, :]).astype(BF16)


def _gather(pos, h2, cap_c, cap_l, with_ctx):
    bn, ne, tt = pos.shape
    d = h2.shape[-1]
    ng = ne // EXP_GROUP
    pos4 = pos.reshape(bn, ng, EXP_GROUP, tt)
    caps = [cap_l] + ([cap_c] if with_ctx else [])
    return pl.pallas_call(
        functools.partial(_gather_kernel, cap_c=cap_c, cap_l=cap_l),
        grid=(bn, ng),
        in_specs=[pl.BlockSpec((None, None, EXP_GROUP, tt), lambda b, g: (b, g, 0, 0)),
                  pl.BlockSpec((None, tt, d), lambda b, g: (b, 0, 0))],
        out_specs=[pl.BlockSpec((None, EXP_GROUP * cap, d), lambda b, g: (b, g, 0)) for cap in caps],
        out_shape=[jax.ShapeDtypeStruct((bn, ne * cap, d), BF16) for cap in caps],
        compiler_params=_params("parallel", "arbitrary"),
        name="moe_gather",
    )(pos4, h2)


FFN_BATCH = 4


def _ffn_kernel(*refs, nseg):
    x_refs = refs[:nseg]
    wg_ref, wu_ref, wd_ref = refs[nseg:nseg + 3]
    y_refs = refs[nseg + 3:2 * nseg + 3]
    wg16, wu16, wd16 = refs[2 * nseg + 3:]

    @pl.when(pl.program_id(1) == 0)
    def _():
        wg16[...] = wg_ref[...].astype(BF16)
        wu16[...] = wu_ref[...].astype(BF16)
        wd16[...] = wd_ref[...].astype(BF16)

    d = x_refs[0].shape[-1]
    rows = [r.shape[0] * r.shape[1] for r in x_refs]
    xs = [r[...].reshape(n, d) for r, n in zip(x_refs, rows)]
    x = xs[0] if nseg == 1 else jnp.concatenate(xs, axis=0)
    g = _dot(x, wg16[...])
    u = _dot(x, wu16[...])
    hid = (g * _sigmoid(g) * u).astype(BF16)
    y = _dot(hid, wd16[...]).astype(BF16)
    r0 = 0
    for y_ref, n in zip(y_refs, rows):
        y_ref[...] = y[r0:r0 + n].reshape(y_ref.shape)
        r0 += n


def _ffn(xs, caps, w_gate, w_up, w_down, layer):
    bn = xs[0].shape[0]
    _, ne, d, f = w_gate.shape
    bb = FFN_BATCH
    x4 = [x.reshape(bn, ne, cap, d) for x, cap in zip(xs, caps)]
    specs = [pl.BlockSpec((bb, None, cap, d), lambda e, b: (b, e, 0, 0)) for cap in caps]
    wspec = lambda s: pl.BlockSpec((None, None) + s, lambda e, b: (layer, e, 0, 0))
    ys = pl.pallas_call(
        functools.partial(_ffn_kernel, nseg=len(xs)),
        grid=(ne, bn // bb),
        in_specs=specs + [wspec((d, f)), wspec((d, f)), wspec((f, d))],
        out_specs=specs,
        out_shape=[jax.ShapeDtypeStruct(x.shape, BF16) for x in x4],
        scratch_shapes=[pltpu.VMEM((d, f), BF16), pltpu.VMEM((d, f), BF16), pltpu.VMEM((f, d), BF16)],
        compiler_params=_params("parallel", "arbitrary"),
        name="moe_ffn",
    )(*x4, w_gate, w_up, w_down)
    return [y.reshape(x.shape) for y, x in zip(ys, xs)]


SCATTER_TILE = 512


def _onehot_cols(post_ref, gatet_ref, cap, tag):
    nrows, ne = post_ref.shape
    slot = lax.broadcasted_iota(I32, (nrows, cap), 1).astype(F32) + tag
    parts = [jnp.where(post_ref[:, e:e + 1] == slot, gatet_ref[:, e:e + 1], 0.0).astype(BF16) for e in range(ne)]
    return jnp.concatenate(parts, axis=1)


def _scatter_lat_kernel(post_ref, gatet_ref, yl_ref, x_ref, ml_ref, fg_ref, o_ref, *, cap, final):
    moe = _dot(_onehot_cols(post_ref, gatet_ref, cap, 0.0), yl_ref[...])
    x2 = x_ref[...] + ml_ref[5:6, :] * moe
    if final:
        x2 = x2 * lax.rsqrt(jnp.mean(x2 * x2, axis=-1, keepdims=True) + EPS) * fg_ref[...]
    o_ref[...] = x2


def _scatter_ctx_kernel(post_ref, gatet_ref, yc_ref, x_ref, mc_ref, o_ref, *, cap, tag):
    moe = _dot(_onehot_cols(post_ref, gatet_ref, cap, tag), yc_ref[...])
    o_ref[...] = x_ref[...] + mc_ref[5:6, :] * moe


def _scatter_lat(post, gatet, yl, x1, mod_l, final_g, cap, final):
    bn, tt, d = x1.shape
    ne = post.shape[-1]
    seq = tt - CHUNK
    tm = SCATTER_TILE
    tok = lambda cols: pl.BlockSpec((None, tm, cols), lambda b, t: (b, t, 0))
    return pl.pallas_call(
        functools.partial(_scatter_lat_kernel, cap=cap, final=final),
        grid=(bn, seq // tm),
        in_specs=[tok(ne), tok(ne), pl.BlockSpec((None,) + yl.shape[1:], lambda b, t: (b, 0, 0)), tok(d),
                  pl.BlockSpec((None, N_MOD, d), lambda b, t: (b, 0, 0)), _full_spec((1, d))],
        out_specs=tok(d),
        out_shape=jax.ShapeDtypeStruct((bn, seq if final else tt, d), F32),
        input_output_aliases={} if final else {3: 0},
        compiler_params=_params("parallel", "arbitrary"),
        name="moe_scatter_final" if final else "moe_scatter_lat",
    )(post, gatet, yl, x1, mod_l, final_g.reshape(1, d))


def _scatter_ctx(post, gatet, yc, x1, mod_c, cap, tag):
    bn, tt, d = x1.shape
    ne = post.shape[-1]
    cblk = tt // CHUNK - 1
    tok = lambda cols: pl.BlockSpec((None, CHUNK, cols), lambda b: (b, cblk, 0))
    return pl.pallas_call(
        functools.partial(_scatter_ctx_kernel, cap=cap, tag=tag),
        grid=(bn,),
        in_specs=[tok(ne), tok(ne), pl.BlockSpec((None,) + yc.shape[1:], lambda b: (b, 0, 0)), tok(d),
                  pl.BlockSpec((N_MOD, d), lambda b: (0, 0))],
        out_specs=tok(d),
        out_shape=jax.ShapeDtypeStruct((bn, tt, d), F32),
        input_output_aliases={3: 0},
        compiler_params=_params("parallel"),
        name="moe_scatter_ctx",
    )(post, gatet, yc, x1, mod_c)


def kernel(x, c, ctx, c_ctx, ada_w, ada_b, norm_mix_g, norm_ffn_g, final_norm_g, mlstm_w_in, mlstm_conv_w,
           mlstm_conv_b, mlstm_igate_b, mlstm_fgate_b, mlstm_head_norm_g, mlstm_w_out, ret_w_in, ret_decay_logit,
           ret_group_norm_g, ret_w_out, moe_router, moe_w_gate, moe_w_up, moe_w_down):
    bn, seq, d = x.shape
    depth = ada_w.shape[0]
    ne = moe_router.shape[-1]
    assert ctx.shape[1] == CHUNK and seq % SCATTER_TILE == 0 and (seq + CHUNK) % TOK_TILE == 0
    cap_c = CAPACITY_FACTOR * CHUNK // ne
    cap_l = CAPACITY_FACTOR * seq // ne
    assert cap_l % LANES == 0
    segs = ((0, seq, cap_l, 0.0), (seq, CHUNK, cap_c, float(cap_l)))

    rows = -(-(bn + 1) // 8) * 8
    cvec = jnp.concatenate([c, c_ctx[None, :], jnp.zeros((rows - bn - 1, d), F32)], axis=0)
    mod = _adaln(cvec, ada_w, ada_b)
    xall = jnp.concatenate([x, ctx], axis=1)

    for i in range(depth):
        last = i == depth - 1
        mod_l = mod[i, :bn].reshape(bn, N_MOD, d)
        mod_c = mod[i, bn].reshape(N_MOD, d)
        j = i // 2
        if i % 2 == 0:
            q, kt, v, o, gi, gf = _inproj_a(xall, norm_mix_g[i], mod_l, mod_c, mlstm_w_in[j])
            bcs, mloc, at = _mlstm_gates(gi, gf, mlstm_igate_b[j], mlstm_fgate_b[j])
            y = _mlstm_scan(q, kt, v, o, bcs, mloc, at, mlstm_conv_w[j], mlstm_conv_b[j], mlstm_head_norm_g[j])
            w_out = mlstm_w_out[j]
        else:
            q, kt, v, g = _inproj_b(xall, norm_mix_g[i], mod_l, mod_c, ret_w_in[j])
            y = _ret_scan(q, kt, v, g, ret_decay_logit[j], ret_group_norm_g[j])
            w_out = ret_w_out[j]
        x1, h2, logits = _outproj(y, xall, w_out, mod_l, mod_c, norm_ffn_g[i], moe_router[i])
        pos, post, gatet = _route(logits, ne, segs)
        caps = [cap_l] if last else [cap_l, cap_c]
        xs = _gather(pos, h2, cap_c, cap_l, not last)
        ys = _ffn(xs, caps, moe_w_gate, moe_w_up, moe_w_down, i)
        xall = _scatter_lat(post, gatet, ys[0], x1, mod_l, final_norm_g, cap_l, last)
        if not last:
            xall = _scatter_ctx(post, gatet, ys[1], xall, mod_c, cap_c, float(cap_l))
    return xall
```

```python
import functools

import jax
import jax.numpy as jnp
from jax import lax
from jax.experimental import pallas as pl
from jax.experimental.pallas import tpu as pltpu

F32 = jnp.float32
BF16 = jnp.bfloat16
I32 = jnp.int32
HIGHEST = lax.Precision.HIGHEST

EPS = 1e-6
N_MOD = 6
CHUNK = 256
TOK_TILE = 768
A_HEADS, A_DK, A_DV = 8, 64, 128
B_HEADS, B_DK, B_DV = 8, 128, 128
CONV_W = 5
CAPACITY_FACTOR = 2
GRID_W = 64
ROPE_BASE = 10000.0
LANES = 128
VMEM_LIMIT = 56 * 1024 * 1024


def _params(*sem):
    return pltpu.CompilerParams(dimension_semantics=sem, vmem_limit_bytes=VMEM_LIMIT)


def _sigmoid(x):
    return 1.0 / (1.0 + jnp.exp(-x))


def _log_sigmoid(x):
    return jnp.minimum(x, 0.0) - jnp.log1p(jnp.exp(-jnp.abs(x)))


def _nt_dot(a, b, **kw):
    return lax.dot_general(a, b, (((1,), (1,)), ((), ())), preferred_element_type=F32, **kw)


def _dot(a, b, **kw):
    return jnp.dot(a, b, preferred_element_type=F32, **kw)


def _rms_mod(x, g, shift, scale):
    y = x * lax.rsqrt(jnp.mean(x * x, axis=-1, keepdims=True) + EPS)
    return (y * g) * (1.0 + scale) + shift


def _tile_mod(mod_l_ref, mod_c_ref, tile, ntiles, rows, sub):
    if (sub + 1) * CHUNK < rows:
        return mod_l_ref[...]
    return jnp.where(tile == ntiles - 1, mod_c_ref[...], mod_l_ref[...])


def _adaln_kernel(c_ref, w_ref, b_ref, o_ref):
    c = c_ref[...]
    s = c * _sigmoid(c)
    o_ref[...] = _dot(s, w_ref[...], precision=HIGHEST) + b_ref[...]


def _adaln(cvec, ada_w, ada_b):
    depth, d, n = ada_w.shape
    rows = cvec.shape[0]
    tn = 1536
    return pl.pallas_call(
        _adaln_kernel,
        grid=(depth, n // tn),
        in_specs=[pl.BlockSpec((rows, d), lambda l, j: (0, 0)),
                  pl.BlockSpec((None, d, tn), lambda l, j: (l, 0, j)),
                  pl.BlockSpec((None, 1, tn), lambda l, j: (l, 0, j))],
        out_specs=pl.BlockSpec((None, rows, tn), lambda l, j: (l, 0, j)),
        out_shape=jax.ShapeDtypeStruct((depth, rows, n), F32),
        compiler_params=_params("parallel", "parallel"),
        name="adaln",
    )(cvec, ada_w, ada_b.reshape(depth, 1, n))


def _inproj_a_kernel(x_ref, g_ref, ml_ref, mc_ref, wq_ref, wkt_ref, wv_ref, wo_ref, wgi_ref, wgf_ref,
                     q_ref, kt_ref, v_ref, o_ref, gi_ref, gf_ref):
    for s in range(x_ref.shape[0] // CHUNK):
        rows = slice(s * CHUNK, (s + 1) * CHUNK)
        mod = _tile_mod(ml_ref, mc_ref, pl.program_id(1), pl.num_programs(1), x_ref.shape[0], s)
        h = _rms_mod(x_ref[rows, :], g_ref[...], mod[0:1, :], mod[1:2, :]).astype(BF16)
        q_ref[rows, :] = _dot(h, wq_ref[...]).astype(BF16)
        kt_ref[:, rows] = _nt_dot(wkt_ref[...], h).astype(BF16)
        v_ref[rows, :] = _dot(h, wv_ref[...]).astype(BF16)
        o_ref[rows, :] = _dot(h, wo_ref[...]).astype(BF16)
        gi_ref[rows, :] = _dot(h, wgi_ref[...])
        gf_ref[rows, :] = _dot(h, wgf_ref[...])


def _inproj_b_kernel(x_ref, g_ref, ml_ref, mc_ref, cs_ref, sn_ref, cst_ref, snt_ref,
                     wq_ref, wkt_ref, wv_ref, wg_ref, q_ref, kt_ref, v_ref, gate_ref):
    lane = lax.broadcasted_iota(I32, (CHUNK, B_DK), 1)
    first = (lane % 64) < 32
    for s in range(x_ref.shape[0] // CHUNK):
        rows = slice(s * CHUNK, (s + 1) * CHUNK)
        mod = _tile_mod(ml_ref, mc_ref, pl.program_id(1), pl.num_programs(1), x_ref.shape[0], s)
        h = _rms_mod(x_ref[rows, :], g_ref[...], mod[0:1, :], mod[1:2, :]).astype(BF16)
        v_ref[rows, :] = _dot(h, wv_ref[...]).astype(BF16)
        gate_ref[rows, :] = _dot(h, wg_ref[...]).astype(BF16)
        cs, sn = cs_ref[rows, :], sn_ref[rows, :]
        q = _dot(h, wq_ref[...])
        for hd in range(B_HEADS):
            qh = q[:, hd * B_DK:(hd + 1) * B_DK]
            partner = jnp.where(first, pltpu.roll(qh, B_DK - 32, axis=1), pltpu.roll(qh, 32, axis=1))
            q_ref[rows, hd * B_DK:(hd + 1) * B_DK] = (qh * cs + partner * sn).astype(BF16)
        kt = _nt_dot(wkt_ref[...], h) * (B_DK ** -0.5)
        cst, snt = cst_ref[:, rows], snt_ref[:, rows]
        for hd in range(B_HEADS):
            for half in range(2):
                base = hd * B_DK + half * 64
                x1 = kt[base:base + 32, :]
                x2 = kt[base + 32:base + 64, :]
                c_ = cst[half * 32:(half + 1) * 32, :]
                s_ = snt[half * 32:(half + 1) * 32, :]
                kt_ref[base:base + 32, rows] = (x1 * c_ - x2 * s_).astype(BF16)
                kt_ref[base + 32:base + 64, rows] = (x1 * s_ + x2 * c_).astype(BF16)


def _tok_spec(cols, tm):
    return pl.BlockSpec((None, tm, cols), lambda b, t: (b, t, 0))


def _full_spec(shape):
    nd = len(shape)
    return pl.BlockSpec(shape, lambda b, t: (0,) * nd, pipeline_mode=pl.Buffered(1))


def _mod_specs(d):
    return [pl.BlockSpec((None, N_MOD, d), lambda b, t: (b, 0, 0)), _full_spec((N_MOD, d))]


def _inproj_a(xall, g, mod_l, mod_c, w_in):
    bn, tt, d = xall.shape
    tm = TOK_TILE
    nq, nv = A_HEADS * A_DK, A_HEADS * A_DV
    wq = w_in[:, :nq].astype(BF16)
    wkt = w_in[:, nq:2 * nq].T.astype(BF16)
    wv = w_in[:, 2 * nq:2 * nq + nv].astype(BF16)
    wo = w_in[:, 2 * nq + nv:2 * nq + 2 * nv].astype(BF16)
    wg = w_in[:, 2 * nq + 2 * nv:]
    pad = jnp.zeros((d, 128 - 2 * A_HEADS), F32)
    perm = jnp.array(_GATE_PERM)
    wgi = jnp.concatenate([wg[:, :2 * A_HEADS][:, perm], pad], axis=1).astype(BF16)
    wgf = jnp.concatenate([wg[:, 2 * A_HEADS:][:, perm], pad], axis=1).astype(BF16)
    ws = [wq, wkt, wv, wo, wgi, wgf]
    return pl.pallas_call(
        _inproj_a_kernel,
        grid=(bn, tt // tm),
        in_specs=[_tok_spec(d, tm), _full_spec((1, d))] + _mod_specs(d) + [_full_spec(w.shape) for w in ws],
        out_specs=[_tok_spec(nq, tm), pl.BlockSpec((None, nq, tm), lambda b, t: (b, 0, t)),
                   _tok_spec(nv, tm), _tok_spec(nv, tm), _tok_spec(128, tm), _tok_spec(128, tm)],
        out_shape=[jax.ShapeDtypeStruct((bn, tt, nq), BF16), jax.ShapeDtypeStruct((bn, nq, tt), BF16),
                   jax.ShapeDtypeStruct((bn, tt, nv), BF16), jax.ShapeDtypeStruct((bn, tt, nv), BF16),
                   jax.ShapeDtypeStruct((bn, tt, 128), F32), jax.ShapeDtypeStruct((bn, tt, 128), F32)],
        compiler_params=_params("parallel", "parallel"),
        name="inproj_mlstm",
    )(xall, g.reshape(1, d), mod_l, mod_c, *ws)


def _rope_tables(tt):
    seq = tt - CHUNK
    rows = seq // GRID_W
    r = jnp.broadcast_to(jnp.arange(rows, dtype=F32)[:, None], (rows, GRID_W)).reshape(-1)
    col = jnp.broadcast_to(jnp.arange(GRID_W, dtype=F32)[None, :], (rows, GRID_W)).reshape(-1)
    nf = B_DK // 4
    inv = ROPE_BASE ** (-jnp.arange(nf, dtype=F32) / nf)
    ang = jnp.concatenate([r[:, None] * inv, col[:, None] * inv], axis=1)
    cos = jnp.concatenate([jnp.cos(ang), jnp.ones((CHUNK, 2 * nf), F32)], axis=0)
    sin = jnp.concatenate([jnp.sin(ang), jnp.zeros((CHUNK, 2 * nf), F32)], axis=0)
    cr, cc, sr, sc = cos[:, :nf], cos[:, nf:], sin[:, :nf], sin[:, nf:]
    cs = jnp.concatenate([cr, cr, cc, cc], axis=1)
    sn = jnp.concatenate([-sr, sr, -sc, sc], axis=1)
    return cs, sn, cos.T, sin.T


def _inproj_b(xall, g, mod_l, mod_c, w_in):
    bn, tt, d = xall.shape
    tm = TOK_TILE
    n = B_HEADS * B_DK
    wq = w_in[:, :n].astype(BF16)
    wkt = w_in[:, n:2 * n].T.astype(BF16)
    wv = w_in[:, 2 * n:3 * n].astype(BF16)
    wg = w_in[:, 3 * n:].astype(BF16)
    cs, sn, cst, snt = _rope_tables(tt)
    ws = [wq, wkt, wv, wg]
    tab_specs = [pl.BlockSpec((tm, B_DK), lambda b, t: (t, 0)), pl.BlockSpec((tm, B_DK), lambda b, t: (t, 0)),
                 pl.BlockSpec((64, tm), lambda b, t: (0, t)), pl.BlockSpec((64, tm), lambda b, t: (0, t))]
    return pl.pallas_call(
        _inproj_b_kernel,
        grid=(bn, tt // tm),
        in_specs=[_tok_spec(d, tm), _full_spec((1, d))] + _mod_specs(d) + tab_specs
                 + [_full_spec(w.shape) for w in ws],
        out_specs=[_tok_spec(n, tm), pl.BlockSpec((None, n, tm), lambda b, t: (b, 0, t)),
                   _tok_spec(n, tm), _tok_spec(n, tm)],
        out_shape=[jax.ShapeDtypeStruct((bn, tt, n), BF16), jax.ShapeDtypeStruct((bn, n, tt), BF16),
                   jax.ShapeDtypeStruct((bn, tt, n), BF16), jax.ShapeDtypeStruct((bn, tt, n), BF16)],
        compiler_params=_params("parallel", "parallel"),
        name="inproj_ret",
    )(xall, g.reshape(1, d), mod_l, mod_c, cs, sn, cst, snt, *ws)


_GATE_PERM = [d * A_HEADS + 2 * hp + hh for hp in range(A_HEADS // 2) for d in range(2) for hh in range(2)]


def _scan_chunks(tt, chunk, reverse):
    n = tt // chunk
    nctx = CHUNK // chunk
    order = list(range(n - nctx, n)) + list(range(n - nctx))
    if reverse:
        order = list(range(n - 1, n - nctx - 1, -1)) + list(range(n - nctx - 1, -1, -1))
    return order


def _chunk_scan(x, pos, op, reverse):
    n = x.shape[0]
    s = 1
    while s < CHUNK:
        if reverse:
            sh = pltpu.roll(x, n - s, axis=0)
            ok = pos < CHUNK - s
        else:
            sh = pltpu.roll(x, s, axis=0)
            ok = pos >= s
        x = jnp.where(ok, op(x, sh), x)
        s *= 2
    return x


def _mlstm_gates_kernel(gi_ref, gf_ref, bi_ref, bf_ref, b_ref, m_ref, at_ref):
    ig = gi_ref[...] + bi_ref[...]
    lf = _log_sigmoid(gf_ref[...] + bf_ref[...])
    tt = ig.shape[0]
    pos = lax.broadcasted_iota(I32, (tt, 1), 0) % CHUNK
    fwd = (lax.broadcasted_iota(I32, (1, ig.shape[1]), 1) // 2) % 2 == 0
    b = jnp.where(fwd, _chunk_scan(lf, pos, jnp.add, False), _chunk_scan(lf, pos, jnp.add, True))
    a = ig - b
    mloc = jnp.where(fwd, _chunk_scan(a, pos, jnp.maximum, False), _chunk_scan(a, pos, jnp.maximum, True))
    for hp in range(A_HEADS // 2):
        b_ref[hp] = b[:, 4 * hp:4 * hp + 4]
        m_ref[hp] = mloc[:, 4 * hp:4 * hp + 4]
    at_ref[...] = a.T[:2 * A_HEADS, :]


def _mlstm_gates(gi, gf, ig_b, fg_b):
    bn, tt, w = gi.shape
    nc = 2 * A_HEADS
    perm = jnp.array(_GATE_PERM)
    padrow = lambda v: jnp.concatenate([v.reshape(nc)[perm].reshape(1, nc).astype(F32),
                                        jnp.zeros((1, w - nc), F32)], axis=1)
    nhp = A_HEADS // 2
    spec = pl.BlockSpec((None, tt, w), lambda b: (b, 0, 0))
    row = pl.BlockSpec((1, w), lambda b: (0, 0))
    col_spec = pl.BlockSpec((None, nhp, tt, 4), lambda b: (b, 0, 0, 0))
    row_spec = pl.BlockSpec((None, nc, tt), lambda b: (b, 0, 0))
    col_shape = jax.ShapeDtypeStruct((bn, nhp, tt, 4), F32)
    row_shape = jax.ShapeDtypeStruct((bn, nc, tt), F32)
    return pl.pallas_call(
        _mlstm_gates_kernel,
        grid=(bn,),
        in_specs=[spec, spec, row, row],
        out_specs=[col_spec, col_spec, row_spec],
        out_shape=[col_shape, col_shape, row_shape],
        compiler_params=_params("parallel"),
        name="mlstm_gates",
    )(gi, gf, padrow(ig_b), padrow(fg_b))


def _seg_valid(idx, off, tt):
    src = idx + off
    ok = (src >= 0) & (src < tt)
    return ok & ((idx < tt - CHUNK) == (src < tt - CHUNK))


def _mlstm_scan_kernel(q_ref, kt_ref, v_ref, o_ref, b_ref, m_ref, at_ref, cwq_ref, cbq_ref, cwk_ref, cbk_ref,
                       ng_ref, y_ref, qs_ref, ks_ref, hs_ref):
    tt = q_ref.shape[0]
    pad = CONV_W // 2
    q = q_ref[...].astype(F32)
    kt = kt_ref[...].astype(F32)
    ridx = lax.broadcasted_iota(I32, (tt, 1), 0)
    lidx = lax.broadcasted_iota(I32, (1, tt), 1)
    qa = q * cwq_ref[pad:pad + 1, :] + cbq_ref[...]
    ka = kt * cwk_ref[:, pad:pad + 1] + cbk_ref[...]
    for w in range(CONV_W):
        off = w - pad
        if off == 0:
            continue
        qsh = pltpu.roll(q, (-off) % tt, axis=0)
        qa = qa + jnp.where(_seg_valid(ridx, off, tt), qsh, 0.0) * cwq_ref[w:w + 1, :]
        ksh = pltpu.roll(kt, (-off) % tt, axis=1)
        ka = ka + jnp.where(_seg_valid(lidx, off, tt), ksh, 0.0) * cwk_ref[:, w:w + 1]
    qs_ref[...] = (qa * _sigmoid(qa) * (A_DK ** -0.5)).astype(BF16)
    ks_ref[...] = (ka * _sigmoid(ka)).astype(BF16)

    ii = lax.broadcasted_iota(I32, (CHUNK, CHUNK), 0)
    jj = lax.broadcasted_iota(I32, (CHUNK, CHUNK), 1)
    ones_col = (lax.broadcasted_iota(I32, (CHUNK, A_DV), 1) == 0).astype(BF16)

    for hh in range(2):
        for d in range(2):
            mask = (jj >= ii) if d else (jj <= ii)
            last = 0 if d else CHUNK - 1
            state = jnp.zeros((A_DK, 2 * A_DV), F32)
            m = jnp.zeros((1, 1), F32)
            for c in _scan_chunks(tt, CHUNK, bool(d)):
                r0 = c * CHUNK
                qc = qs_ref[r0:r0 + CHUNK, hh * A_DK:(hh + 1) * A_DK]
                ktc = ks_ref[hh * A_DK:(hh + 1) * A_DK, r0:r0 + CHUNK]
                vaug = jnp.concatenate([v_ref[r0:r0 + CHUNK, hh * A_DV:(hh + 1) * A_DV], ones_col], axis=1)
                col = slice(2 * d + hh, 2 * d + hh + 1)
                a_row = at_ref[col, r0:r0 + CHUNK]
                mcol = jnp.maximum(m, m_ref[r0:r0 + CHUNK, col])
                bcol = b_ref[r0:r0 + CHUNK, col]
                decay = jnp.where(mask, jnp.exp(a_row - mcol), 0.0)
                s = (_dot(qc, ktc) * decay).astype(BF16)
                w_inter = jnp.exp(m - mcol)
                nd = _dot(s, vaug) + w_inter * _dot(qc, state.astype(BF16))
                num = nd[:, :A_DV]
                den = nd[:, A_DV:A_DV + 1]
                hout = num / jnp.maximum(jnp.abs(den), jnp.exp(-(bcol + mcol)))
                dst = (slice(r0, r0 + CHUNK), slice(hh * A_DV, (hh + 1) * A_DV))
                if d:
                    hs_ref[dst] = hs_ref[dst] + hout
                else:
                    hs_ref[dst] = hout
                m_last = mcol[last:last + 1, :]
                wk = jnp.exp(a_row - m_last)
                state = jnp.exp(m - m_last) * state + _dot((ktc.astype(F32) * wk).astype(BF16), vaug)
                m = bcol[last:last + 1, :] + m_last
    for hh in range(2):
        sl = slice(hh * A_DV, (hh + 1) * A_DV)
        y = hs_ref[:, sl]
        y = y * lax.rsqrt(jnp.mean(y * y, axis=-1, keepdims=True) + EPS)
        y = y * ng_ref[:, sl] * _sigmoid(o_ref[:, sl].astype(F32))
        y_ref[:, sl] = y.astype(BF16)


def _mlstm_scan(q, kt, v, o, bcs, mloc, at, conv_w, conv_b, norm_g):
    bn, tt, nq = q.shape
    nv = v.shape[-1]
    nhp = A_HEADS // 2
    col_spec = pl.BlockSpec((None, None, tt, 4), lambda b, h: (b, h, 0, 0))
    row_spec = pl.BlockSpec((None, None, 4, tt), lambda b, h: (b, h, 0, 0))
    cwq = conv_w[:, :nq]
    cwk = conv_w[:, nq:].T
    cbq = conv_b[:nq].reshape(1, nq)
    cbk = conv_b[nq:].reshape(nq, 1)
    return pl.pallas_call(
        _mlstm_scan_kernel,
        grid=(bn, A_HEADS // 2),
        in_specs=[pl.BlockSpec((None, tt, 2 * A_DK), lambda b, h: (b, 0, h)),
                  pl.BlockSpec((None, 2 * A_DK, tt), lambda b, h: (b, h, 0)),
                  pl.BlockSpec((None, tt, 2 * A_DV), lambda b, h: (b, 0, h)),
                  pl.BlockSpec((None, tt, 2 * A_DV), lambda b, h: (b, 0, h))]
                 + [col_spec, col_spec, row_spec] + [
                  pl.BlockSpec((CONV_W, 2 * A_DK), lambda b, h: (0, h)),
                  pl.BlockSpec((1, 2 * A_DK), lambda b, h: (0, h)),
                  pl.BlockSpec((2 * A_DK, CONV_W), lambda b, h: (h, 0)),
                  pl.BlockSpec((2 * A_DK, 1), lambda b, h: (h, 0)),
                  pl.BlockSpec((1, 2 * A_DV), lambda b, h: (0, h))],
        out_specs=pl.BlockSpec((None, tt, 2 * A_DV), lambda b, h: (b, 0, h)),
        out_shape=jax.ShapeDtypeStruct((bn, tt, nv), BF16),
        scratch_shapes=[pltpu.VMEM((tt, 2 * A_DK), BF16), pltpu.VMEM((2 * A_DK, tt), BF16),
                        pltpu.VMEM((tt, 2 * A_DV), F32)],
        compiler_params=_params("parallel", "parallel"),
        name="mlstm_scan",
    )(q, kt, v, o, bcs, mloc, at.reshape(bn, nhp, 4, tt), cwq, cbq, cwk, cbk, norm_g.reshape(1, nv))


def _ret_scan_kernel(q_ref, kt_ref, v_ref, g_ref, dl0_ref, dl1_ref, gn_ref, y_ref, st_ref):
    tt = q_ref.shape[0]
    L = CHUNK
    ii = lax.broadcasted_iota(I32, (L, L), 0)
    jj = lax.broadcasted_iota(I32, (L, L), 1)
    ci = lax.broadcasted_iota(I32, (L, 1), 0).astype(F32)
    rj = lax.broadcasted_iota(I32, (1, L), 1).astype(F32)
    logg = [_log_sigmoid(dl0_ref[...]), _log_sigmoid(dl1_ref[...])]
    for d in range(2):
        order = _scan_chunks(tt, L, bool(d))
        kdec = jnp.exp((rj if d else L - 1.0 - rj) * logg[d])
        sdec = jnp.exp(L * logg[d])
        state = jnp.zeros((B_DK, B_DV), F32)
        for n, c in enumerate(order):
            rows = slice(c * L, (c + 1) * L)
            st_ref[d, c] = state.astype(BF16)
            if n + 1 < len(order):
                state = sdec * state + _dot((kt_ref[:, rows].astype(F32) * kdec).astype(BF16), v_ref[rows, :])
    dist = (ii - jj).astype(F32)
    dcomb = (jnp.where(ii >= jj, jnp.exp(dist * logg[0]), 0.0)
             + jnp.where(jj >= ii, jnp.exp(-dist * logg[1]), 0.0))
    inter_f = jnp.exp((ci + 1.0) * logg[0])
    inter_b = jnp.exp((L - ci) * logg[1])
    for c in range(tt // L):
        rows = slice(c * L, (c + 1) * L)
        qc = q_ref[rows, :]
        q32 = qc.astype(F32)
        s = (_dot(qc, kt_ref[:, rows]) * dcomb).astype(BF16)
        lhs = jnp.concatenate([s, (q32 * inter_f).astype(BF16), (q32 * inter_b).astype(BF16)], axis=1)
        rhs = jnp.concatenate([v_ref[rows, :], st_ref[0, c], st_ref[1, c]], axis=0)
        y = _dot(lhs, rhs)
        mu = jnp.mean(y, axis=-1, keepdims=True)
        yc = y - mu
        var = jnp.mean(yc * yc, axis=-1, keepdims=True)
        g = g_ref[rows, :].astype(F32)
        y_ref[rows, :] = (yc * lax.rsqrt(var + EPS) * gn_ref[...] * (g * _sigmoid(g))).astype(BF16)


def _ret_scan(q, kt, v, g, decay_logit, gn_g):
    bn, tt, n = q.shape
    dl = decay_logit.astype(F32).reshape(2, B_HEADS, 1, 1)
    tok = pl.BlockSpec((None, tt, B_DK), lambda b, h: (b, 0, h))
    return pl.pallas_call(
        _ret_scan_kernel,
        grid=(bn, B_HEADS),
        in_specs=[tok, pl.BlockSpec((None, B_DK, tt), lambda b, h: (b, h, 0)), tok, tok,
                  pl.BlockSpec((None, None, 1, 1), lambda b, h: (0, h, 0, 0)),
                  pl.BlockSpec((None, None, 1, 1), lambda b, h: (1, h, 0, 0)),
                  pl.BlockSpec((1, B_DV), lambda b, h: (0, h))],
        out_specs=tok,
        out_shape=jax.ShapeDtypeStruct((bn, tt, n), BF16),
        scratch_shapes=[pltpu.VMEM((2, tt // CHUNK, B_DK, B_DV), BF16)],
        compiler_params=_params("parallel", "parallel"),
        name="ret_scan",
    )(q, kt, v, g, dl, dl, gn_g.reshape(1, n))


def _outproj_kernel(y_ref, x_ref, w_ref, ml_ref, mc_ref, g_ref, r2_ref, x1_ref, h2_ref, lg_ref):
    for s in range(x_ref.shape[0] // CHUNK):
        rows = slice(s * CHUNK, (s + 1) * CHUNK)
        mod = _tile_mod(ml_ref, mc_ref, pl.program_id(1), pl.num_programs(1), x_ref.shape[0], s)
        out = _dot(y_ref[rows, :], w_ref[...])
        x1 = x_ref[rows, :] + mod[2:3, :] * out
        x1_ref[rows, :] = x1
        h2 = _rms_mod(x1, g_ref[...], mod[3:4, :], mod[4:5, :])
        hi = h2.astype(BF16)
        h2_ref[rows, :] = hi
        lo = (h2 - hi.astype(F32)).astype(BF16)
        a = _dot(hi, r2_ref[...])
        lg_ref[rows, :] = (a[:, :LANES] + a[:, LANES:]) + _dot(lo, r2_ref[:, :LANES])


def _outproj(y, xall, w_out, mod_l, mod_c, g, router):
    bn, tt, d = xall.shape
    tm = TOK_TILE
    ne = router.shape[1]
    r = jnp.concatenate([router.astype(F32), jnp.zeros((d, LANES - ne), F32)], axis=1)
    rhi = r.astype(BF16)
    r2 = jnp.concatenate([rhi, (r - rhi.astype(F32)).astype(BF16)], axis=1)
    return pl.pallas_call(
        _outproj_kernel,
        grid=(bn, tt // tm),
        in_specs=[_tok_spec(y.shape[-1], tm), _tok_spec(d, tm), _full_spec(w_out.shape)] + _mod_specs(d)
                 + [_full_spec((1, d)), _full_spec((d, 2 * LANES))],
        out_specs=[_tok_spec(d, tm), _tok_spec(d, tm), _tok_spec(LANES, tm)],
        out_shape=[jax.ShapeDtypeStruct((bn, tt, d), F32), jax.ShapeDtypeStruct((bn, tt, d), BF16),
                   jax.ShapeDtypeStruct((bn, tt, LANES), F32)],
        compiler_params=_params("parallel", "parallel"),
        name="outproj_router",
    )(y, xall, w_out.astype(BF16), mod_l, mod_c, g.reshape(1, d), r2)


def _lane_cumsum(x, upper):
    n = x.shape[1]
    parts = []
    carry = jnp.zeros((x.shape[0], 1), F32)
    for j in range(n // CHUNK):
        cs = _dot(x[:, j * CHUNK:(j + 1) * CHUNK].astype(BF16), upper) + carry
        parts.append(cs)
        carry = cs[:, CHUNK - 1:CHUNK]
    return parts[0] if len(parts) == 1 else jnp.concatenate(parts, axis=1)


def _kth_largest_bits(bits_list, caps):
    def count(bits, cand):
        return jnp.sum((bits >= cand).astype(F32), axis=1, keepdims=True)

    taus = [jnp.zeros((b.shape[0], 1), I32) for b in bits_list]
    for p, nbits in [(30, 1)] + [(p, 2) for p in range(28, -1, -2)]:
        for i, (bits, cap) in enumerate(zip(bits_list, caps)):
            tau = taus[i]
            digit = sum((count(bits, tau + (j << p)) >= cap).astype(I32) for j in range(1, 1 << nbits))
            taus[i] = tau + digit * (1 << p)
    return taus


def _route_kernel(lg_ref, pos_ref, post_ref, gatet_ref, gate_ref, *, segs):
    ne = pos_ref.shape[0]
    si = lax.broadcasted_iota(I32, (CHUNK, CHUNK), 0)
    ti = lax.broadcasted_iota(I32, (CHUNK, CHUNK), 1)
    upper = (si <= ti).astype(BF16)
    lgt = lg_ref[...].T[:ne, :]
    affs = []
    for (s0, n, cap, tag) in segs:
        lg = lgt[:, s0:s0 + n]
        e = jnp.exp(lg - jnp.max(lg, axis=0, keepdims=True))
        affs.append(e / jnp.sum(e, axis=0, keepdims=True))
    bits_list = [pltpu.bitcast(aff, I32) for aff in affs]
    taus = _kth_largest_bits(bits_list, [cap for (_, _, cap, _) in segs])
    for (s0, n, cap, tag), aff, bits, tau in zip(segs, affs, bits_list, taus):
        gt = bits > tau
        eq = bits == tau
        need = cap - jnp.sum(gt.astype(F32), axis=1, keepdims=True)
        cum_eq = _lane_cumsum(eq.astype(F32), upper)
        sel = gt | (eq & (cum_eq <= need))
        pos = _lane_cumsum(sel.astype(F32), upper) - 1.0 + tag
        pos_ref[:, s0:s0 + n] = jnp.where(sel, pos, -1.0)
        gate_ref[:, s0:s0 + n] = aff
    pad = jnp.zeros((LANES - ne, pos_ref.shape[1]), F32)
    post_ref[...] = jnp.concatenate([pos_ref[...], pad], axis=0).T[:, :ne]
    gatet_ref[...] = jnp.concatenate([gate_ref[...], pad], axis=0).T[:, :ne]


def _route(logits, ne, segs):
    bn, tt, w = logits.shape
    row = pl.BlockSpec((None, ne, tt), lambda b: (b, 0, 0))
    col = pl.BlockSpec((None, tt, ne), lambda b: (b, 0, 0))
    return pl.pallas_call(
        functools.partial(_route_kernel, segs=segs),
        grid=(bn,),
        in_specs=[pl.BlockSpec((None, tt, w), lambda b: (b, 0, 0))],
        out_specs=[row, col, col],
        out_shape=[jax.ShapeDtypeStruct((bn, ne, tt), F32),
                   jax.ShapeDtypeStruct((bn, tt, ne), F32), jax.ShapeDtypeStruct((bn, tt, ne), F32)],
        scratch_shapes=[pltpu.VMEM((ne, tt), F32)],
        compiler_params=_params("parallel"),
        name="route",
    )(logits)


EXP_GROUP = 4


def _onehot_rows(pos_ref, lo, hi, cap, tag):
    parts = []
    slot = lax.broadcasted_iota(I32, (cap, hi - lo), 0).astype(F32) + tag
    for r in range(EXP_GROUP):
        parts.append(jnp.where(pos_ref[r:r + 1, lo:hi] == slot, 1.0, 0.0).astype(BF16))
    return jnp.concatenate(parts, axis=0)


def _gather_kernel(pos_ref, h_ref, xl_ref, *maybe_xc_ref, cap_c, cap_l):
    tt = h_ref.shape[0]
    seq = tt - CHUNK
    xl_ref[...] = _dot(_onehot_rows(pos_ref, 0, seq, cap_l, 0.0), h_ref[0:seq, :]).astype(BF16)
    for xc_ref in maybe_xc_ref:
        xc_ref[...] = _dot(_onehot_rows(pos_ref, seq, tt, cap_c, float(cap_l)), h_ref[seq:tt, :]).astype(BF16)


def _gather(pos, h2, cap_c, cap_l, with_ctx):
    bn, ne, tt = pos.shape
    d = h2.shape[-1]
    ng = ne // EXP_GROUP
    pos4 = pos.reshape(bn, ng, EXP_GROUP, tt)
    caps = [cap_l] + ([cap_c] if with_ctx else [])
    return pl.pallas_call(
        functools.partial(_gather_kernel, cap_c=cap_c, cap_l=cap_l),
        grid=(bn, ng),
        in_specs=[pl.BlockSpec((None, None, EXP_GROUP, tt), lambda b, g: (b, g, 0, 0)),
                  pl.BlockSpec((None, tt, d), lambda b, g: (b, 0, 0))],
        out_specs=[pl.BlockSpec((None, EXP_GROUP * cap, d), lambda b, g: (b, g, 0)) for cap in caps],
        out_shape=[jax.ShapeDtypeStruct((bn, ne * cap, d), BF16) for cap in caps],
        compiler_params=_params("parallel", "arbitrary"),
        name="moe_gather",
    )(pos4, h2)


FFN_BATCH = 4


def _ffn_kernel(*refs, nseg):
    x_refs = refs[:nseg]
    wg_ref, wu_ref, wd_ref = refs[nseg:nseg + 3]
    y_refs = refs[nseg + 3:2 * nseg + 3]
    wg16, wu16, wd16 = refs[2 * nseg + 3:]

    @pl.when(pl.program_id(1) == 0)
    def _():
        wg16[...] = wg_ref[...].astype(BF16)
        wu16[...] = wu_ref[...].astype(BF16)
        wd16[...] = wd_ref[...].astype(BF16)

    d = x_refs[0].shape[-1]
    rows = [r.shape[0] * r.shape[1] for r in x_refs]
    xs = [r[...].reshape(n, d) for r, n in zip(x_refs, rows)]
    x = xs[0] if nseg == 1 else jnp.concatenate(xs, axis=0)
    g = _dot(x, wg16[...])
    u = _dot(x, wu16[...])
    hid = (g * _sigmoid(g) * u).astype(BF16)
    y = _dot(hid, wd16[...]).astype(BF16)
    r0 = 0
    for y_ref, n in zip(y_refs, rows):
        y_ref[...] = y[r0:r0 + n].reshape(y_ref.shape)
        r0 += n


def _ffn(xs, caps, w_gate, w_up, w_down, layer):
    bn = xs[0].shape[0]
    _, ne, d, f = w_gate.shape
    bb = FFN_BATCH
    x4 = [x.reshape(bn, ne, cap, d) for x, cap in zip(xs, caps)]
    specs = [pl.BlockSpec((bb, None, cap, d), lambda e, b: (b, e, 0, 0)) for cap in caps]
    wspec = lambda s: pl.BlockSpec((None, None) + s, lambda e, b: (layer, e, 0, 0))
    ys = pl.pallas_call(
        functools.partial(_ffn_kernel, nseg=len(xs)),
        grid=(ne, bn // bb),
        in_specs=specs + [wspec((d, f)), wspec((d, f)), wspec((f, d))],
        out_specs=specs,
        out_shape=[jax.ShapeDtypeStruct(x.shape, BF16) for x in x4],
        scratch_shapes=[pltpu.VMEM((d, f), BF16), pltpu.VMEM((d, f), BF16), pltpu.VMEM((f, d), BF16)],
        compiler_params=_params("parallel", "arbitrary"),
        name="moe_ffn",
    )(*x4, w_gate, w_up, w_down)
    return [y.reshape(x.shape) for y, x in zip(ys, xs)]


SCATTER_TILE = 512


def _onehot_cols(post_ref, gatet_ref, cap, tag):
    nrows, ne = post_ref.shape
    slot = lax.broadcasted_iota(I32, (nrows, cap), 1).astype(F32) + tag
    parts = [jnp.where(post_ref[:, e:e + 1] == slot, gatet_ref[:, e:e + 1], 0.0).astype(BF16) for e in range(ne)]
    return jnp.concatenate(parts, axis=1)


def _scatter_lat_kernel(post_ref, gatet_ref, yl_ref, x_ref, ml_ref, fg_ref, o_ref, *, cap, final):
    moe = _dot(_onehot_cols(post_ref, gatet_ref, cap, 0.0), yl_ref[...])
    x2 = x_ref[...] + ml_ref[5:6, :] * moe
    if final:
        x2 = x2 * lax.rsqrt(jnp.mean(x2 * x2, axis=-1, keepdims=True) + EPS) * fg_ref[...]
    o_ref[...] = x2


def _scatter_ctx_kernel(post_ref, gatet_ref, yc_ref, x_ref, mc_ref, o_ref, *, cap, tag):
    moe = _dot(_onehot_cols(post_ref, gatet_ref, cap, tag), yc_ref[...])
    o_ref[...] = x_ref[...] + mc_ref[5:6, :] * moe


def _scatter_lat(post, gatet, yl, x1, mod_l, final_g, cap, final):
    bn, tt, d = x1.shape
    ne = post.shape[-1]
    seq = tt - CHUNK
    tm = SCATTER_TILE
    tok = lambda cols: pl.BlockSpec((None, tm, cols), lambda b, t: (b, t, 0))
    return pl.pallas_call(
        functools.partial(_scatter_lat_kernel, cap=cap, final=final),
        grid=(bn, seq // tm),
        in_specs=[tok(ne), tok(ne), pl.BlockSpec((None,) + yl.shape[1:], lambda b, t: (b, 0, 0)), tok(d),
                  pl.BlockSpec((None, N_MOD, d), lambda b, t: (b, 0, 0)), _full_spec((1, d))],
        out_specs=tok(d),
        out_shape=jax.ShapeDtypeStruct((bn, seq if final else tt, d), F32),
        input_output_aliases={} if final else {3: 0},
        compiler_params=_params("parallel", "arbitrary"),
        name="moe_scatter_final" if final else "moe_scatter_lat",
    )(post, gatet, yl, x1, mod_l, final_g.reshape(1, d))


def _scatter_ctx(post, gatet, yc, x1, mod_c, cap, tag):
    bn, tt, d = x1.shape
    ne = post.shape[-1]
    cblk = tt // CHUNK - 1
    tok = lambda cols: pl.BlockSpec((None, CHUNK, cols), lambda b: (b, cblk, 0))
    return pl.pallas_call(
        functools.partial(_scatter_ctx_kernel, cap=cap, tag=tag),
        grid=(bn,),
        in_specs=[tok(ne), tok(ne), pl.BlockSpec((None,) + yc.shape[1:], lambda b: (b, 0, 0)), tok(d),
                  pl.BlockSpec((N_MOD, d), lambda b: (0, 0))],
        out_specs=tok(d),
        out_shape=jax.ShapeDtypeStruct((bn, tt, d), F32),
        input_output_aliases={3: 0},
        compiler_params=_params("parallel"),
        name="moe_scatter_ctx",
    )(post, gatet, yc, x1, mod_c)


def kernel(x, c, ctx, c_ctx, ada_w, ada_b, norm_mix_g, norm_ffn_g, final_norm_g, mlstm_w_in, mlstm_conv_w,
           mlstm_conv_b, mlstm_igate_b, mlstm_fgate_b, mlstm_head_norm_g, mlstm_w_out, ret_w_in, ret_decay_logit,
           ret_group_norm_g, ret_w_out, moe_router, moe_w_gate, moe_w_up, moe_w_down):
    bn, seq, d = x.shape
    depth = ada_w.shape[0]
    ne = moe_router.shape[-1]
    assert ctx.shape[1] == CHUNK and seq % SCATTER_TILE == 0 and (seq + CHUNK) % TOK_TILE == 0
    cap_c = CAPACITY_FACTOR * CHUNK // ne
    cap_l = CAPACITY_FACTOR * seq // ne
    assert cap_l % LANES == 0
    segs = ((0, seq, cap_l, 0.0), (seq, CHUNK, cap_c, float(cap_l)))

    rows = -(-(bn + 1) // 8) * 8
    cvec = jnp.concatenate([c, c_ctx[None, :], jnp.zeros((rows - bn - 1, d), F32)], axis=0)
    mod = _adaln(cvec, ada_w, ada_b)
    xall = jnp.concatenate([x, ctx], axis=1)

    for i in range(depth):
        last = i == depth - 1
        mod_l = mod[i, :bn].reshape(bn, N_MOD, d)
        mod_c = mod[i, bn].reshape(N_MOD, d)
        j = i // 2
        if i % 2 == 0:
            q, kt, v, o, gi, gf = _inproj_a(xall, norm_mix_g[i], mod_l, mod_c, mlstm_w_in[j])
            bcs, mloc, at = _mlstm_gates(gi, gf, mlstm_igate_b[j], mlstm_fgate_b[j])
            y = _mlstm_scan(q, kt, v, o, bcs, mloc, at, mlstm_conv_w[j], mlstm_conv_b[j], mlstm_head_norm_g[j])
            w_out = mlstm_w_out[j]
        else:
            q, kt, v, g = _inproj_b(xall, norm_mix_g[i], mod_l, mod_c, ret_w_in[j])
            y = _ret_scan(q, kt, v, g, ret_decay_logit[j], ret_group_norm_g[j])
            w_out = ret_w_out[j]
        x1, h2, logits = _outproj(y, xall, w_out, mod_l, mod_c, norm_ffn_g[i], moe_router[i])
        pos, post, gatet = _route(logits, ne, segs)
        caps = [cap_l] if last else [cap_l, cap_c]
        xs = _gather(pos, h2, cap_c, cap_l, not last)
        ys = _ffn(xs, caps, moe_w_gate, moe_w_up, moe_w_down, i)
        xall = _scatter_lat(post, gatet, ys[0], x1, mod_l, final_norm_g, cap_l, last)
        if not last:
            xall = _scatter_ctx(post, gatet, ys[1], xall, mod_c, cap_c, float(cap_l))
    return xall
```

```python
import functools

import jax
import jax.numpy as jnp
from jax import lax
from jax.experimental import pallas as pl
from jax.experimental.pallas import tpu as pltpu

F32 = jnp.float32
BF16 = jnp.bfloat16
I32 = jnp.int32
HIGHEST = lax.Precision.HIGHEST

EPS = 1e-6
N_MOD = 6
CHUNK = 256
TOK_TILE = 768
A_HEADS, A_DK, A_DV = 8, 64, 128
B_HEADS, B_DK, B_DV = 8, 128, 128
CONV_W = 5
CAPACITY_FACTOR = 2
GRID_W = 64
ROPE_BASE = 10000.0
LANES = 128
VMEM_LIMIT = 56 * 1024 * 1024


def _params(*sem):
    return pltpu.CompilerParams(dimension_semantics=sem, vmem_limit_bytes=VMEM_LIMIT)


def _sigmoid(x):
    return 0.5 * jnp.tanh(0.5 * x) + 0.5


def _log_sigmoid(x):
    return jnp.minimum(x, 0.0) - jnp.log1p(jnp.exp(-jnp.abs(x)))


def _nt_dot(a, b, **kw):
    return lax.dot_general(a, b, (((1,), (1,)), ((), ())), preferred_element_type=F32, **kw)


def _dot(a, b, **kw):
    return jnp.dot(a, b, preferred_element_type=F32, **kw)


def _rms_mod(x, g, shift, scale):
    y = x * lax.rsqrt(jnp.mean(x * x, axis=-1, keepdims=True) + EPS)
    return (y * g) * (1.0 + scale) + shift


def _tile_mod(mod_l_ref, mod_c_ref, tile, ntiles, rows, sub):
    if (sub + 1) * CHUNK < rows:
        return mod_l_ref[...]
    return jnp.where(tile == ntiles - 1, mod_c_ref[...], mod_l_ref[...])


def _adaln_kernel(c_ref, w_ref, b_ref, o_ref):
    c = c_ref[...]
    s = c * _sigmoid(c)
    o_ref[...] = _dot(s, w_ref[...], precision=HIGHEST) + b_ref[...]


def _adaln(cvec, ada_w, ada_b):
    depth, d, n = ada_w.shape
    rows = cvec.shape[0]
    tn = 1536
    return pl.pallas_call(
        _adaln_kernel,
        grid=(depth, n // tn),
        in_specs=[pl.BlockSpec((rows, d), lambda l, j: (0, 0)),
                  pl.BlockSpec((None, d, tn), lambda l, j: (l, 0, j)),
                  pl.BlockSpec((None, 1, tn), lambda l, j: (l, 0, j))],
        out_specs=pl.BlockSpec((None, rows, tn), lambda l, j: (l, 0, j)),
        out_shape=jax.ShapeDtypeStruct((depth, rows, n), F32),
        compiler_params=_params("parallel", "parallel"),
        name="adaln",
    )(cvec, ada_w, ada_b.reshape(depth, 1, n))


def _inproj_a_kernel(x_ref, g_ref, ml_ref, mc_ref, wq_ref, wk_ref, wv_ref, wo_ref, wgi_ref, wgf_ref,
                     q_ref, k_ref, v_ref, o_ref, gi_ref, gf_ref):
    for s in range(x_ref.shape[0] // CHUNK):
        rows = slice(s * CHUNK, (s + 1) * CHUNK)
        mod = _tile_mod(ml_ref, mc_ref, pl.program_id(1), pl.num_programs(1), x_ref.shape[0], s)
        h = _rms_mod(x_ref[rows, :], g_ref[...], mod[0:1, :], mod[1:2, :]).astype(BF16)
        q_ref[rows, :] = _dot(h, wq_ref[...]).astype(BF16)
        k_ref[rows, :] = _dot(h, wk_ref[...]).astype(BF16)
        v_ref[rows, :] = _dot(h, wv_ref[...]).astype(BF16)
        o_ref[rows, :] = _dot(h, wo_ref[...]).astype(BF16)
        gi_ref[rows, :] = _dot(h, wgi_ref[...])
        gf_ref[rows, :] = _dot(h, wgf_ref[...])


def _inproj_b_kernel(x_ref, g_ref, ml_ref, mc_ref, cs_ref, sn_ref, cst_ref, snt_ref,
                     wq_ref, wkt_ref, wv_ref, wg_ref, q_ref, kt_ref, v_ref, gate_ref):
    lane = lax.broadcasted_iota(I32, (CHUNK, B_DK), 1)
    first = (lane % 64) < 32
    for s in range(x_ref.shape[0] // CHUNK):
        rows = slice(s * CHUNK, (s + 1) * CHUNK)
        mod = _tile_mod(ml_ref, mc_ref, pl.program_id(1), pl.num_programs(1), x_ref.shape[0], s)
        h = _rms_mod(x_ref[rows, :], g_ref[...], mod[0:1, :], mod[1:2, :]).astype(BF16)
        v_ref[rows, :] = _dot(h, wv_ref[...]).astype(BF16)
        gate_ref[rows, :] = _dot(h, wg_ref[...]).astype(BF16)
        cs, sn = cs_ref[rows, :], sn_ref[rows, :]
        q = _dot(h, wq_ref[...])
        for hd in range(B_HEADS):
            qh = q[:, hd * B_DK:(hd + 1) * B_DK]
            partner = jnp.where(first, pltpu.roll(qh, B_DK - 32, axis=1), pltpu.roll(qh, 32, axis=1))
            q_ref[rows, hd * B_DK:(hd + 1) * B_DK] = (qh * cs + partner * sn).astype(BF16)
        kt = _nt_dot(wkt_ref[...], h) * (B_DK ** -0.5)
        cst, snt = cst_ref[:, rows], snt_ref[:, rows]
        for hd in range(B_HEADS):
            for half in range(2):
                base = hd * B_DK + half * 64
                x1 = kt[base:base + 32, :]
                x2 = kt[base + 32:base + 64, :]
                c_ = cst[half * 32:(half + 1) * 32, :]
                s_ = snt[half * 32:(half + 1) * 32, :]
                kt_ref[base:base + 32, rows] = (x1 * c_ - x2 * s_).astype(BF16)
                kt_ref[base + 32:base + 64, rows] = (x1 * s_ + x2 * c_).astype(BF16)


def _tok_spec(cols, tm):
    return pl.BlockSpec((None, tm, cols), lambda b, t: (b, t, 0))


def _full_spec(shape):
    nd = len(shape)
    return pl.BlockSpec(shape, lambda b, t: (0,) * nd, pipeline_mode=pl.Buffered(1))


def _mod_specs(d):
    return [pl.BlockSpec((None, N_MOD, d), lambda b, t: (b, 0, 0)), _full_spec((N_MOD, d))]


def _inproj_a(xall, g, mod_l, mod_c, w_in):
    bn, tt, d = xall.shape
    tm = TOK_TILE
    nq, nv = A_HEADS * A_DK, A_HEADS * A_DV
    wq = w_in[:, :nq].astype(BF16)
    wk = w_in[:, nq:2 * nq].astype(BF16)
    wv = w_in[:, 2 * nq:2 * nq + nv].astype(BF16)
    wo = w_in[:, 2 * nq + nv:2 * nq + 2 * nv].astype(BF16)
    wg = w_in[:, 2 * nq + 2 * nv:]
    pad = jnp.zeros((d, 128 - 2 * A_HEADS), F32)
    perm = jnp.array(_GATE_PERM)
    wgi = jnp.concatenate([wg[:, :2 * A_HEADS][:, perm], pad], axis=1).astype(BF16)
    wgf = jnp.concatenate([wg[:, 2 * A_HEADS:][:, perm], pad], axis=1).astype(BF16)
    ws = [wq, wk, wv, wo, wgi, wgf]
    return pl.pallas_call(
        _inproj_a_kernel,
        grid=(bn, tt // tm),
        in_specs=[_tok_spec(d, tm), _full_spec((1, d))] + _mod_specs(d) + [_full_spec(w.shape) for w in ws],
        out_specs=[_tok_spec(nq, tm), _tok_spec(nq, tm),
                   _tok_spec(nv, tm), _tok_spec(nv, tm), _tok_spec(128, tm), _tok_spec(128, tm)],
        out_shape=[jax.ShapeDtypeStruct((bn, tt, nq), BF16), jax.ShapeDtypeStruct((bn, tt, nq), BF16),
                   jax.ShapeDtypeStruct((bn, tt, nv), BF16), jax.ShapeDtypeStruct((bn, tt, nv), BF16),
                   jax.ShapeDtypeStruct((bn, tt, 128), F32), jax.ShapeDtypeStruct((bn, tt, 128), F32)],
        compiler_params=_params("parallel", "parallel"),
        name="inproj_mlstm",
    )(xall, g.reshape(1, d), mod_l, mod_c, *ws)


def _rope_tables(tt):
    seq = tt - CHUNK
    rows = seq // GRID_W
    r = jnp.broadcast_to(jnp.arange(rows, dtype=F32)[:, None], (rows, GRID_W)).reshape(-1)
    col = jnp.broadcast_to(jnp.arange(GRID_W, dtype=F32)[None, :], (rows, GRID_W)).reshape(-1)
    nf = B_DK // 4
    inv = ROPE_BASE ** (-jnp.arange(nf, dtype=F32) / nf)
    ang = jnp.concatenate([r[:, None] * inv, col[:, None] * inv], axis=1)
    cos = jnp.concatenate([jnp.cos(ang), jnp.ones((CHUNK, 2 * nf), F32)], axis=0)
    sin = jnp.concatenate([jnp.sin(ang), jnp.zeros((CHUNK, 2 * nf), F32)], axis=0)
    cr, cc, sr, sc = cos[:, :nf], cos[:, nf:], sin[:, :nf], sin[:, nf:]
    cs = jnp.concatenate([cr, cr, cc, cc], axis=1)
    sn = jnp.concatenate([-sr, sr, -sc, sc], axis=1)
    return cs, sn, cos.T, sin.T


def _inproj_b(xall, g, mod_l, mod_c, w_in):
    bn, tt, d = xall.shape
    tm = TOK_TILE
    n = B_HEADS * B_DK
    wq = w_in[:, :n].astype(BF16)
    wkt = w_in[:, n:2 * n].T.astype(BF16)
    wv = w_in[:, 2 * n:3 * n].astype(BF16)
    wg = w_in[:, 3 * n:].astype(BF16)
    cs, sn, cst, snt = _rope_tables(tt)
    ws = [wq, wkt, wv, wg]
    tab_specs = [pl.BlockSpec((tm, B_DK), lambda b, t: (t, 0)), pl.BlockSpec((tm, B_DK), lambda b, t: (t, 0)),
                 pl.BlockSpec((64, tm), lambda b, t: (0, t)), pl.BlockSpec((64, tm), lambda b, t: (0, t))]
    return pl.pallas_call(
        _inproj_b_kernel,
        grid=(bn, tt // tm),
        in_specs=[_tok_spec(d, tm), _full_spec((1, d))] + _mod_specs(d) + tab_specs
                 + [_full_spec(w.shape) for w in ws],
        out_specs=[_tok_spec(n, tm), pl.BlockSpec((None, n, tm), lambda b, t: (b, 0, t)),
                   _tok_spec(n, tm), _tok_spec(n, tm)],
        out_shape=[jax.ShapeDtypeStruct((bn, tt, n), BF16), jax.ShapeDtypeStruct((bn, n, tt), BF16),
                   jax.ShapeDtypeStruct((bn, tt, n), BF16), jax.ShapeDtypeStruct((bn, tt, n), BF16)],
        compiler_params=_params("parallel", "parallel"),
        name="inproj_ret",
    )(xall, g.reshape(1, d), mod_l, mod_c, cs, sn, cst, snt, *ws)


_GATE_PERM = [d * A_HEADS + 2 * hp + hh for hp in range(A_HEADS // 2) for d in range(2) for hh in range(2)]


def _scan_chunks(tt, chunk, reverse):
    n = tt // chunk
    nctx = CHUNK // chunk
    order = list(range(n - nctx, n)) + list(range(n - nctx))
    if reverse:
        order = list(range(n - 1, n - nctx - 1, -1)) + list(range(n - nctx - 1, -1, -1))
    return order


def _chunk_scan(x, pos, op, reverse):
    n = x.shape[0]
    s = 1
    while s < CHUNK:
        if reverse:
            sh = pltpu.roll(x, n - s, axis=0)
            ok = pos < CHUNK - s
        else:
            sh = pltpu.roll(x, s, axis=0)
            ok = pos >= s
        x = jnp.where(ok, op(x, sh), x)
        s *= 2
    return x


def _mlstm_gates_kernel(gi_ref, gf_ref, bi_ref, bf_ref, b_ref, m_ref, at_ref):
    ig = gi_ref[...] + bi_ref[...]
    lf = _log_sigmoid(gf_ref[...] + bf_ref[...])
    tt = ig.shape[0]
    pos = lax.broadcasted_iota(I32, (tt, 1), 0) % CHUNK
    fwd = (lax.broadcasted_iota(I32, (1, ig.shape[1]), 1) // 2) % 2 == 0
    b = jnp.where(fwd, _chunk_scan(lf, pos, jnp.add, False), _chunk_scan(lf, pos, jnp.add, True))
    a = ig - b
    mloc = jnp.where(fwd, _chunk_scan(a, pos, jnp.maximum, False), _chunk_scan(a, pos, jnp.maximum, True))
    for hp in range(A_HEADS // 2):
        b_ref[hp] = b[:, 4 * hp:4 * hp + 4]
        m_ref[hp] = mloc[:, 4 * hp:4 * hp + 4]
    at_ref[...] = a.T[:2 * A_HEADS, :]


def _mlstm_gates(gi, gf, ig_b, fg_b):
    bn, tt, w = gi.shape
    nc = 2 * A_HEADS
    perm = jnp.array(_GATE_PERM)
    padrow = lambda v: jnp.concatenate([v.reshape(nc)[perm].reshape(1, nc).astype(F32),
                                        jnp.zeros((1, w - nc), F32)], axis=1)
    nhp = A_HEADS // 2
    spec = pl.BlockSpec((None, tt, w), lambda b: (b, 0, 0))
    row = pl.BlockSpec((1, w), lambda b: (0, 0))
    col_spec = pl.BlockSpec((None, nhp, tt, 4), lambda b: (b, 0, 0, 0))
    row_spec = pl.BlockSpec((None, nc, tt), lambda b: (b, 0, 0))
    col_shape = jax.ShapeDtypeStruct((bn, nhp, tt, 4), F32)
    row_shape = jax.ShapeDtypeStruct((bn, nc, tt), F32)
    return pl.pallas_call(
        _mlstm_gates_kernel,
        grid=(bn,),
        in_specs=[spec, spec, row, row],
        out_specs=[col_spec, col_spec, row_spec],
        out_shape=[col_shape, col_shape, row_shape],
        compiler_params=_params("parallel"),
        name="mlstm_gates",
    )(gi, gf, padrow(ig_b), padrow(fg_b))


CONV_PAD = 8


def _mlstm_scan_kernel(q_ref, k_ref, v_ref, o_ref, b_ref, m_ref, at_ref, cwq_ref, cbq_ref, cwk_ref, cbk_ref,
                       ng_ref, y_ref, xq_ref, xk_ref, qs_ref, ks_ref, hs_ref):
    tt = q_ref.shape[0]
    seq = tt - CHUNK
    half = CONV_W // 2
    zeros = jnp.zeros((CONV_PAD, q_ref.shape[1]), F32)
    bases = []
    for c in range(tt // CHUNK):
        r0 = c * CHUNK
        base = CONV_PAD + r0 + (2 * CONV_PAD if r0 >= seq else 0)
        bases.append(base)
        xq_ref[base:base + CHUNK, :] = q_ref[r0:r0 + CHUNK, :].astype(F32)
        xk_ref[base:base + CHUNK, :] = k_ref[r0:r0 + CHUNK, :].astype(F32)
    for x_ref in (xq_ref, xk_ref):
        x_ref[0:CONV_PAD, :] = zeros
        x_ref[CONV_PAD + seq:3 * CONV_PAD + seq, :] = jnp.concatenate([zeros, zeros], axis=0)
        x_ref[3 * CONV_PAD + tt:4 * CONV_PAD + tt, :] = zeros
    for c, base in enumerate(bases):
        rows = slice(c * CHUNK, (c + 1) * CHUNK)
        qa = cbq_ref[...]
        ka = cbk_ref[...]
        for w in range(CONV_W):
            qa = qa + xq_ref[base + w - half:base + w - half + CHUNK, :] * cwq_ref[w:w + 1, :]
            ka = ka + xk_ref[base + w - half:base + w - half + CHUNK, :] * cwk_ref[w:w + 1, :]
        qs_ref[rows, :] = (qa * _sigmoid(qa) * (A_DK ** -0.5)).astype(BF16)
        ks_ref[:, rows] = (ka * _sigmoid(ka)).T.astype(BF16)

    ii = lax.broadcasted_iota(I32, (CHUNK, CHUNK), 0)
    jj = lax.broadcasted_iota(I32, (CHUNK, CHUNK), 1)
    ones_col = (lax.broadcasted_iota(I32, (CHUNK, A_DV), 1) == 0).astype(BF16)

    for hh in range(2):
        for d in range(2):
            mask = (jj >= ii) if d else (jj <= ii)
            last = 0 if d else CHUNK - 1
            state = jnp.zeros((A_DK, 2 * A_DV), F32)
            m = jnp.zeros((1, 1), F32)
            for c in _scan_chunks(tt, CHUNK, bool(d)):
                r0 = c * CHUNK
                qc = qs_ref[r0:r0 + CHUNK, hh * A_DK:(hh + 1) * A_DK]
                ktc = ks_ref[hh * A_DK:(hh + 1) * A_DK, r0:r0 + CHUNK]
                vaug = jnp.concatenate([v_ref[r0:r0 + CHUNK, hh * A_DV:(hh + 1) * A_DV], ones_col], axis=1)
                col = slice(2 * d + hh, 2 * d + hh + 1)
                a_row = at_ref[col, r0:r0 + CHUNK]
                mcol = jnp.maximum(m, m_ref[r0:r0 + CHUNK, col])
                bcol = b_ref[r0:r0 + CHUNK, col]
                decay = jnp.where(mask, jnp.exp(a_row - mcol), 0.0)
                s = (_dot(qc, ktc) * decay).astype(BF16)
                w_inter = jnp.exp(m - mcol)
                nd = _dot(s, vaug) + w_inter * _dot(qc, state.astype(BF16))
                num = nd[:, :A_DV]
                den = nd[:, A_DV:A_DV + 1]
                hout = num / jnp.maximum(jnp.abs(den), jnp.exp(-(bcol + mcol)))
                dst = (slice(r0, r0 + CHUNK), slice(hh * A_DV, (hh + 1) * A_DV))
                if d:
                    hs_ref[dst] = hs_ref[dst] + hout
                else:
                    hs_ref[dst] = hout
                m_last = mcol[last:last + 1, :]
                wk = jnp.exp(a_row - m_last)
                state = jnp.exp(m - m_last) * state + _dot((ktc.astype(F32) * wk).astype(BF16), vaug)
                m = bcol[last:last + 1, :] + m_last
    for c in range(tt // CHUNK):
        rows = slice(c * CHUNK, (c + 1) * CHUNK)
        for hh in range(2):
            sl = slice(hh * A_DV, (hh + 1) * A_DV)
            y = hs_ref[rows, sl]
            y = y * lax.rsqrt(jnp.mean(y * y, axis=-1, keepdims=True) + EPS)
            y = y * ng_ref[:, sl] * _sigmoid(o_ref[rows, sl].astype(F32))
            y_ref[rows, sl] = y.astype(BF16)


def _mlstm_scan(q, k, v, o, bcs, mloc, at, conv_w, conv_b, norm_g):
    bn, tt, nq = q.shape
    nv = v.shape[-1]
    nhp = A_HEADS // 2
    col_spec = pl.BlockSpec((None, None, tt, 4), lambda b, h: (b, h, 0, 0))
    row_spec = pl.BlockSpec((None, None, 4, tt), lambda b, h: (b, h, 0, 0))
    cwq = conv_w[:, :nq]
    cwk = conv_w[:, nq:]
    cbq = conv_b[:nq].reshape(1, nq)
    cbk = conv_b[nq:].reshape(1, nq)
    return pl.pallas_call(
        _mlstm_scan_kernel,
        grid=(bn, A_HEADS // 2),
        in_specs=[pl.BlockSpec((None, tt, 2 * A_DK), lambda b, h: (b, 0, h)),
                  pl.BlockSpec((None, tt, 2 * A_DK), lambda b, h: (b, 0, h)),
                  pl.BlockSpec((None, tt, 2 * A_DV), lambda b, h: (b, 0, h)),
                  pl.BlockSpec((None, tt, 2 * A_DV), lambda b, h: (b, 0, h))]
                 + [col_spec, col_spec, row_spec] + [
                  pl.BlockSpec((CONV_W, 2 * A_DK), lambda b, h: (0, h)),
                  pl.BlockSpec((1, 2 * A_DK), lambda b, h: (0, h)),
                  pl.BlockSpec((CONV_W, 2 * A_DK), lambda b, h: (0, h)),
                  pl.BlockSpec((1, 2 * A_DK), lambda b, h: (0, h)),
                  pl.BlockSpec((1, 2 * A_DV), lambda b, h: (0, h))],
        out_specs=pl.BlockSpec((None, tt, 2 * A_DV), lambda b, h: (b, 0, h)),
        out_shape=jax.ShapeDtypeStruct((bn, tt, nv), BF16),
        scratch_shapes=[pltpu.VMEM((tt + 4 * CONV_PAD, 2 * A_DK), F32), pltpu.VMEM((tt + 4 * CONV_PAD, 2 * A_DK), F32),
                        pltpu.VMEM((tt, 2 * A_DK), BF16), pltpu.VMEM((2 * A_DK, tt), BF16),
                        pltpu.VMEM((tt, 2 * A_DV), F32)],
        compiler_params=_params("parallel", "parallel"),
        name="mlstm_scan",
    )(q, k, v, o, bcs, mloc, at.reshape(bn, nhp, 4, tt), cwq, cbq, cwk, cbk, norm_g.reshape(1, nv))


RET_HEADS = 2


def _ret_scan_kernel(q_ref, kt_ref, v_ref, g_ref, dl_ref, gn_ref, y_ref, st_ref):
    tt = q_ref.shape[0]
    L = CHUNK
    ii = lax.broadcasted_iota(I32, (L, L), 0)
    jj = lax.broadcasted_iota(I32, (L, L), 1)
    ci = lax.broadcasted_iota(I32, (L, 1), 0).astype(F32)
    rj = lax.broadcasted_iota(I32, (1, L), 1).astype(F32)
    dist = (ii - jj).astype(F32)
    for hh in range(RET_HEADS):
        cols = slice(hh * B_DK, (hh + 1) * B_DK)
        logg = [_log_sigmoid(dl_ref[0, hh]), _log_sigmoid(dl_ref[1, hh])]
        for d in range(2):
            order = _scan_chunks(tt, L, bool(d))
            kdec = jnp.exp((rj if d else L - 1.0 - rj) * logg[d])
            sdec = jnp.exp(L * logg[d])
            state = jnp.zeros((B_DK, B_DV), F32)
            for n, c in enumerate(order):
                rows = slice(c * L, (c + 1) * L)
                st_ref[hh, d, c] = state.astype(BF16)
                if n + 1 < len(order):
                    state = sdec * state + _dot((kt_ref[cols, rows].astype(F32) * kdec).astype(BF16), v_ref[rows, cols])
        dcomb = (jnp.where(ii >= jj, jnp.exp(dist * logg[0]), 0.0)
                 + jnp.where(jj >= ii, jnp.exp(-dist * logg[1]), 0.0))
        inter_f = jnp.exp((ci + 1.0) * logg[0])
        inter_b = jnp.exp((L - ci) * logg[1])
        for c in range(tt // L):
            rows = slice(c * L, (c + 1) * L)
            qc = q_ref[rows, cols]
            q32 = qc.astype(F32)
            s = (_dot(qc, kt_ref[cols, rows]) * dcomb).astype(BF16)
            lhs = jnp.concatenate([s, (q32 * inter_f).astype(BF16), (q32 * inter_b).astype(BF16)], axis=1)
            rhs = jnp.concatenate([v_ref[rows, cols], st_ref[hh, 0, c], st_ref[hh, 1, c]], axis=0)
            y = _dot(lhs, rhs)
            mu = jnp.mean(y, axis=-1, keepdims=True)
            yc = y - mu
            var = jnp.mean(yc * yc, axis=-1, keepdims=True)
            g = g_ref[rows, cols].astype(F32)
            y_ref[rows, cols] = (yc * lax.rsqrt(var + EPS) * gn_ref[:, cols] * (g * _sigmoid(g))).astype(BF16)


def _ret_scan(q, kt, v, g, decay_logit, gn_g):
    bn, tt, n = q.shape
    nh = RET_HEADS
    dl = decay_logit.astype(F32).reshape(2, B_HEADS // nh, nh, 1, 1).transpose(1, 0, 2, 3, 4)
    tok = pl.BlockSpec((None, tt, nh * B_DK), lambda b, h: (b, 0, h))
    return pl.pallas_call(
        _ret_scan_kernel,
        grid=(bn, B_HEADS // nh),
        in_specs=[tok, pl.BlockSpec((None, nh * B_DK, tt), lambda b, h: (b, h, 0)), tok, tok,
                  pl.BlockSpec((None, 2, nh, 1, 1), lambda b, h: (h, 0, 0, 0, 0)),
                  pl.BlockSpec((1, nh * B_DV), lambda b, h: (0, h))],
        out_specs=tok,
        out_shape=jax.ShapeDtypeStruct((bn, tt, n), BF16),
        scratch_shapes=[pltpu.VMEM((nh, 2, tt // CHUNK, B_DK, B_DV), BF16)],
        compiler_params=_params("parallel", "parallel"),
        name="ret_scan",
    )(q, kt, v, g, dl, gn_g.reshape(1, n))


def _outproj_kernel(y_ref, x_ref, w_ref, ml_ref, mc_ref, g_ref, r2_ref, x1_ref, h2_ref, lg_ref):
    for s in range(x_ref.shape[0] // CHUNK):
        rows = slice(s * CHUNK, (s + 1) * CHUNK)
        mod = _tile_mod(ml_ref, mc_ref, pl.program_id(1), pl.num_programs(1), x_ref.shape[0], s)
        out = _dot(y_ref[rows, :], w_ref[...])
        x1 = x_ref[rows, :] + mod[2:3, :] * out
        x1_ref[rows, :] = x1
        h2 = _rms_mod(x1, g_ref[...], mod[3:4, :], mod[4:5, :])
        hi = h2.astype(BF16)
        h2_ref[rows, :] = hi
        lo = (h2 - hi.astype(F32)).astype(BF16)
        a = _dot(hi, r2_ref[...])
        lg_ref[rows, :] = (a[:, :LANES] + a[:, LANES:]) + _dot(lo, r2_ref[:, :LANES])


def _outproj(y, xall, w_out, mod_l, mod_c, g, router):
    bn, tt, d = xall.shape
    tm = TOK_TILE
    ne = router.shape[1]
    r = jnp.concatenate([router.astype(F32), jnp.zeros((d, LANES - ne), F32)], axis=1)
    rhi = r.astype(BF16)
    r2 = jnp.concatenate([rhi, (r - rhi.astype(F32)).astype(BF16)], axis=1)
    return pl.pallas_call(
        _outproj_kernel,
        grid=(bn, tt // tm),
        in_specs=[_tok_spec(y.shape[-1], tm), _tok_spec(d, tm), _full_spec(w_out.shape)] + _mod_specs(d)
                 + [_full_spec((1, d)), _full_spec((d, 2 * LANES))],
        out_specs=[_tok_spec(d, tm), _tok_spec(d, tm), _tok_spec(LANES, tm)],
        out_shape=[jax.ShapeDtypeStruct((bn, tt, d), F32), jax.ShapeDtypeStruct((bn, tt, d), BF16),
                   jax.ShapeDtypeStruct((bn, tt, LANES), F32)],
        compiler_params=_params("parallel", "parallel"),
        name="outproj_router",
    )(y, xall, w_out.astype(BF16), mod_l, mod_c, g.reshape(1, d), r2)


def _lane_cumsum(x, upper):
    n = x.shape[1]
    parts = []
    carry = jnp.zeros((x.shape[0], 1), F32)
    for j in range(n // CHUNK):
        cs = _dot(x[:, j * CHUNK:(j + 1) * CHUNK].astype(BF16), upper) + carry
        parts.append(cs)
        carry = cs[:, CHUNK - 1:CHUNK]
    return parts[0] if len(parts) == 1 else jnp.concatenate(parts, axis=1)


def _kth_largest(vals, caps):
    def count(x, cand_bits):
        return jnp.sum((x >= pltpu.bitcast(cand_bits, F32)).astype(F32), axis=1, keepdims=True)

    taus = [jnp.zeros((x.shape[0], 1), I32) for x in vals]
    for p, nbits in [(30, 1)] + [(p, 2) for p in range(28, -1, -2)]:
        for i, (x, cap) in enumerate(zip(vals, caps)):
            tau = taus[i]
            digit = sum((count(x, tau + (j << p)) >= cap).astype(I32) for j in range(1, 1 << nbits))
            taus[i] = tau + digit * (1 << p)
    return [pltpu.bitcast(tau, F32) for tau in taus]


ROUTE_BATCH = 2


def _route_kernel(lg_ref, pos_ref, post_ref, gatet_ref, gate_ref, *, segs):
    nb, ne = pos_ref.shape[0], pos_ref.shape[1]
    si = lax.broadcasted_iota(I32, (CHUNK, CHUNK), 0)
    ti = lax.broadcasted_iota(I32, (CHUNK, CHUNK), 1)
    upper = (si <= ti).astype(BF16)
    work = []
    for i in range(nb):
        lgt = lg_ref[i].T[:ne, :]
        for seg in segs:
            s0, n, cap, tag = seg
            lg = lgt[:, s0:s0 + n]
            e = jnp.exp(lg - jnp.max(lg, axis=0, keepdims=True))
            work.append((i, seg, e / jnp.sum(e, axis=0, keepdims=True)))
    taus = _kth_largest([aff for (_, _, aff) in work], [seg[2] for (_, seg, _) in work])
    for (i, (s0, n, cap, tag), aff), tau in zip(work, taus):
        gt = aff > tau
        eq = aff == tau
        need = cap - jnp.sum(gt.astype(F32), axis=1, keepdims=True)
        cum_eq = _lane_cumsum(eq.astype(F32), upper)
        sel = gt | (eq & (cum_eq <= need))
        pos = _lane_cumsum(sel.astype(F32), upper) - 1.0 + tag
        pos_ref[i, :, s0:s0 + n] = jnp.where(sel, pos, -1.0)
        gate_ref[i, :, s0:s0 + n] = aff
    pad = jnp.zeros((LANES - ne, pos_ref.shape[2]), F32)
    for i in range(nb):
        post_ref[i] = jnp.concatenate([pos_ref[i], pad], axis=0).T[:, :ne]
        gatet_ref[i] = jnp.concatenate([gate_ref[i], pad], axis=0).T[:, :ne]


def _route(logits, ne, segs):
    bn, tt, w = logits.shape
    nb = ROUTE_BATCH
    row = pl.BlockSpec((nb, ne, tt), lambda b: (b, 0, 0))
    col = pl.BlockSpec((nb, tt, ne), lambda b: (b, 0, 0))
    return pl.pallas_call(
        functools.partial(_route_kernel, segs=segs),
        grid=(bn // nb,),
        in_specs=[pl.BlockSpec((nb, tt, w), lambda b: (b, 0, 0))],
        out_specs=[row, col, col],
        out_shape=[jax.ShapeDtypeStruct((bn, ne, tt), F32),
                   jax.ShapeDtypeStruct((bn, tt, ne), F32), jax.ShapeDtypeStruct((bn, tt, ne), F32)],
        scratch_shapes=[pltpu.VMEM((nb, ne, tt), F32)],
        compiler_params=_params("parallel"),
        name="route",
    )(logits)


EXP_GROUP = 4


def _onehot_rows(pos_ref, lo, hi, cap, tag):
    parts = []
    slot = lax.broadcasted_iota(I32, (cap, hi - lo), 0).astype(F32) + tag
    for r in range(EXP_GROUP):
        parts.append(jnp.where(pos_ref[r:r + 1, lo:hi] == slot, 1.0, 0.0).astype(BF16))
    return jnp.concatenate(parts, axis=0)


def _gather_kernel(pos_ref, h_ref, xl_ref, *maybe_xc_ref, cap_c, cap_l):
    tt = h_ref.shape[0]
    seq = tt - CHUNK
    xl_ref[...] = _dot(_onehot_rows(pos_ref, 0, seq, cap_l, 0.0), h_ref[0:seq, :]).astype(BF16)
    for xc_ref in maybe_xc_ref:
        xc_ref[...] = _dot(_onehot_rows(pos_ref, seq, tt, cap_c, float(cap_l)), h_ref[seq:tt, :]).astype(BF16)


def _gather(pos, h2, cap_c, cap_l, with_ctx):
    bn, ne, tt = pos.shape
    d = h2.shape[-1]
    ng = ne // EXP_GROUP
    pos4 = pos.reshape(bn, ng, EXP_GROUP, tt)
    caps = [cap_l] + ([cap_c] if with_ctx else [])
    return pl.pallas_call(
        functools.partial(_gather_kernel, cap_c=cap_c, cap_l=cap_l),
        grid=(bn, ng),
        in_specs=[pl.BlockSpec((None, None, EXP_GROUP, tt), lambda b, g: (b, g, 0, 0)),
                  pl.BlockSpec((None, tt, d), lambda b, g: (b, 0, 0))],
        out_specs=[pl.BlockSpec((None, EXP_GROUP * cap, d), lambda b, g: (b, g, 0)) for cap in caps],
        out_shape=[jax.ShapeDtypeStruct((bn, ne * cap, d), BF16) for cap in caps],
        compiler_params=_params("parallel", "arbitrary"),
        name="moe_gather",
    )(pos4, h2)


FFN_BATCH = 4


def _ffn_kernel(*refs, nseg):
    x_refs = refs[:nseg]
    wg_ref, wu_ref, wd_ref = refs[nseg:nseg + 3]
    y_refs = refs[nseg + 3:2 * nseg + 3]
    wg16, wu16, wd16 = refs[2 * nseg + 3:]

    @pl.when(pl.program_id(1) == 0)
    def _():
        wg16[...] = wg_ref[...].astype(BF16)
        wu16[...] = wu_ref[...].astype(BF16)
        wd16[...] = wd_ref[...].astype(BF16)

    d = x_refs[0].shape[-1]
    rows = [r.shape[0] * r.shape[1] for r in x_refs]
    xs = [r[...].reshape(n, d) for r, n in zip(x_refs, rows)]
    x = xs[0] if nseg == 1 else jnp.concatenate(xs, axis=0)
    g = _dot(x, wg16[...])
    u = _dot(x, wu16[...])
    hid = (g * _sigmoid(g) * u).astype(BF16)
    y = _dot(hid, wd16[...]).astype(BF16)
    r0 = 0
    for y_ref, n in zip(y_refs, rows):
        y_ref[...] = y[r0:r0 + n].reshape(y_ref.shape)
        r0 += n


def _ffn(xs, caps, w_gate, w_up, w_down, layer):
    bn = xs[0].shape[0]
    _, ne, d, f = w_gate.shape
    bb = FFN_BATCH
    x4 = [x.reshape(bn, ne, cap, d) for x, cap in zip(xs, caps)]
    specs = [pl.BlockSpec((bb, None, cap, d), lambda e, b: (b, e, 0, 0)) for cap in caps]
    wspec = lambda s: pl.BlockSpec((None, None) + s, lambda e, b: (layer, e, 0, 0))
    ys = pl.pallas_call(
        functools.partial(_ffn_kernel, nseg=len(xs)),
        grid=(ne, bn // bb),
        in_specs=specs + [wspec((d, f)), wspec((d, f)), wspec((f, d))],
        out_specs=specs,
        out_shape=[jax.ShapeDtypeStruct(x.shape, BF16) for x in x4],
        scratch_shapes=[pltpu.VMEM((d, f), BF16), pltpu.VMEM((d, f), BF16), pltpu.VMEM((f, d), BF16)],
        compiler_params=_params("parallel", "arbitrary"),
        name="moe_ffn",
    )(*x4, w_gate, w_up, w_down)
    return [y.reshape(x.shape) for y, x in zip(ys, xs)]


SCATTER_TILE = 512


def _onehot_cols(post_ref, gatet_ref, cap, tag):
    nrows, ne = post_ref.shape
    slot = lax.broadcasted_iota(I32, (nrows, cap), 1).astype(F32) + tag
    parts = [jnp.where(post_ref[:, e:e + 1] == slot, gatet_ref[:, e:e + 1], 0.0).astype(BF16) for e in range(ne)]
    return jnp.concatenate(parts, axis=1)


def _scatter_lat_kernel(post_ref, gatet_ref, yl_ref, x_ref, ml_ref, fg_ref, o_ref, *, cap, final):
    moe = _dot(_onehot_cols(post_ref, gatet_ref, cap, 0.0), yl_ref[...])
    x2 = x_ref[...] + ml_ref[5:6, :] * moe
    if final:
        x2 = x2 * lax.rsqrt(jnp.mean(x2 * x2, axis=-1, keepdims=True) + EPS) * fg_ref[...]
    o_ref[...] = x2


def _scatter_ctx_kernel(post_ref, gatet_ref, yc_ref, x_ref, mc_ref, o_ref, *, cap, tag):
    moe = _dot(_onehot_cols(post_ref, gatet_ref, cap, tag), yc_ref[...])
    o_ref[...] = x_ref[...] + mc_ref[5:6, :] * moe


def _scatter_lat(post, gatet, yl, x1, mod_l, final_g, cap, final):
    bn, tt, d = x1.shape
    ne = post.shape[-1]
    seq = tt - CHUNK
    tm = SCATTER_TILE
    tok = lambda cols: pl.BlockSpec((None, tm, cols), lambda b, t: (b, t, 0))
    return pl.pallas_call(
        functools.partial(_scatter_lat_kernel, cap=cap, final=final),
        grid=(bn, seq // tm),
        in_specs=[tok(ne), tok(ne), pl.BlockSpec((None,) + yl.shape[1:], lambda b, t: (b, 0, 0)), tok(d),
                  pl.BlockSpec((None, N_MOD, d), lambda b, t: (b, 0, 0)), _full_spec((1, d))],
        out_specs=tok(d),
        out_shape=jax.ShapeDtypeStruct((bn, seq if final else tt, d), F32),
        input_output_aliases={} if final else {3: 0},
        compiler_params=_params("parallel", "arbitrary"),
        name="moe_scatter_final" if final else "moe_scatter_lat",
    )(post, gatet, yl, x1, mod_l, final_g.reshape(1, d))


def _scatter_ctx(post, gatet, yc, x1, mod_c, cap, tag):
    bn, tt, d = x1.shape
    ne = post.shape[-1]
    cblk = tt // CHUNK - 1
    tok = lambda cols: pl.BlockSpec((None, CHUNK, cols), lambda b: (b, cblk, 0))
    return pl.pallas_call(
        functools.partial(_scatter_ctx_kernel, cap=cap, tag=tag),
        grid=(bn,),
        in_specs=[tok(ne), tok(ne), pl.BlockSpec((None,) + yc.shape[1:], lambda b: (b, 0, 0)), tok(d),
                  pl.BlockSpec((N_MOD, d), lambda b: (0, 0))],
        out_specs=tok(d),
        out_shape=jax.ShapeDtypeStruct((bn, tt, d), F32),
        input_output_aliases={3: 0},
        compiler_params=_params("parallel"),
        name="moe_scatter_ctx",
    )(post, gatet, yc, x1, mod_c)


def kernel(x, c, ctx, c_ctx, ada_w, ada_b, norm_mix_g, norm_ffn_g, final_norm_g, mlstm_w_in, mlstm_conv_w,
           mlstm_conv_b, mlstm_igate_b, mlstm_fgate_b, mlstm_head_norm_g, mlstm_w_out, ret_w_in, ret_decay_logit,
           ret_group_norm_g, ret_w_out, moe_router, moe_w_gate, moe_w_up, moe_w_down):
    bn, seq, d = x.shape
    depth = ada_w.shape[0]
    ne = moe_router.shape[-1]
    assert ctx.shape[1] == CHUNK and seq % SCATTER_TILE == 0 and (seq + CHUNK) % TOK_TILE == 0
    cap_c = CAPACITY_FACTOR * CHUNK // ne
    cap_l = CAPACITY_FACTOR * seq // ne
    assert cap_l % LANES == 0
    segs = ((0, seq, cap_l, 0.0), (seq, CHUNK, cap_c, float(cap_l)))

    rows = -(-(bn + 1) // 8) * 8
    cvec = jnp.concatenate([c, c_ctx[None, :], jnp.zeros((rows - bn - 1, d), F32)], axis=0)
    mod = _adaln(cvec, ada_w, ada_b)
    xall = jnp.concatenate([x, ctx], axis=1)

    for i in range(depth):
        last = i == depth - 1
        mod_l = mod[i, :bn].reshape(bn, N_MOD, d)
        mod_c = mod[i, bn].reshape(N_MOD, d)
        j = i // 2
        if i % 2 == 0:
            q, kt, v, o, gi, gf = _inproj_a(xall, norm_mix_g[i], mod_l, mod_c, mlstm_w_in[j])
            bcs, mloc, at = _mlstm_gates(gi, gf, mlstm_igate_b[j], mlstm_fgate_b[j])
            y = _mlstm_scan(q, kt, v, o, bcs, mloc, at, mlstm_conv_w[j], mlstm_conv_b[j], mlstm_head_norm_g[j])
            w_out = mlstm_w_out[j]
        else:
            q, kt, v, g = _inproj_b(xall, norm_mix_g[i], mod_l, mod_c, ret_w_in[j])
            y = _ret_scan(q, kt, v, g, ret_decay_logit[j], ret_group_norm_g[j])
            w_out = ret_w_out[j]
        x1, h2, logits = _outproj(y, xall, w_out, mod_l, mod_c, norm_ffn_g[i], moe_router[i])
        pos, post, gatet = _route(logits, ne, segs)
        caps = [cap_l] if last else [cap_l, cap_c]
        xs = _gather(pos, h2, cap_c, cap_l, not last)
        ys = _ffn(xs, caps, moe_w_gate, moe_w_up, moe_w_down, i)
        xall = _scatter_lat(post, gatet, ys[0], x1, mod_l, final_norm_g, cap_l, last)
        if not last:
            xall = _scatter_ctx(post, gatet, ys[1], xall, mod_c, cap_c, float(cap_l))
    return xall
```

```python
import functools

import jax
import jax.numpy as jnp
from jax import lax
from jax.experimental import pallas as pl
from jax.experimental.pallas import tpu as pltpu

F32 = jnp.float32
BF16 = jnp.bfloat16
I32 = jnp.int32

EPS = 1e-6
N_MOD = 6
CHUNK = 256
TOK_TILE = 768
A_HEADS, A_DK, A_DV = 8, 64, 128
B_HEADS, B_DK, B_DV = 8, 128, 128
CONV_W = 5
CAPACITY_FACTOR = 2
GRID_W = 64
ROPE_BASE = 10000.0
LANES = 128
VMEM_LIMIT = 56 * 1024 * 1024


def _params(*sem):
    return pltpu.CompilerParams(dimension_semantics=sem, vmem_limit_bytes=VMEM_LIMIT)


def _sigmoid(x):
    return 0.5 * jnp.tanh(0.5 * x) + 0.5


def _log_sigmoid(x):
    return jnp.minimum(x, 0.0) - jnp.log1p(jnp.exp(-jnp.abs(x)))


def _nt_dot(a, b, **kw):
    return lax.dot_general(a, b, (((1,), (1,)), ((), ())), preferred_element_type=F32, **kw)


def _dot(a, b, **kw):
    return jnp.dot(a, b, preferred_element_type=F32, **kw)


def _rms_mod(x, g, shift, scale):
    y = x * lax.rsqrt(jnp.mean(x * x, axis=-1, keepdims=True) + EPS)
    return (y * g) * (1.0 + scale) + shift


def _tile_mod(mod_l_ref, mod_c_ref, tile, ntiles, rows, sub):
    if (sub + 1) * CHUNK < rows:
        return mod_l_ref[...]
    return jnp.where(tile == ntiles - 1, mod_c_ref[...], mod_l_ref[...])


def _adaln_kernel(c_ref, w_ref, b_ref, o_ref):
    c = c_ref[...]
    s = c * _sigmoid(c)
    w = w_ref[...]
    s_hi = s.astype(BF16)
    w_hi = w.astype(BF16)
    s_lo = (s - s_hi.astype(F32)).astype(BF16)
    w_lo = (w - w_hi.astype(F32)).astype(BF16)
    o_ref[...] = _dot(s_hi, w_hi) + (_dot(s_lo, w_hi) + _dot(s_hi, w_lo)) + b_ref[...]


def _adaln(cvec, ada_w, ada_b):
    depth, d, n = ada_w.shape
    rows = cvec.shape[0]
    tn = 1536
    return pl.pallas_call(
        _adaln_kernel,
        grid=(depth, n // tn),
        in_specs=[pl.BlockSpec((rows, d), lambda l, j: (0, 0)),
                  pl.BlockSpec((None, d, tn), lambda l, j: (l, 0, j)),
                  pl.BlockSpec((None, 1, tn), lambda l, j: (l, 0, j))],
        out_specs=pl.BlockSpec((None, rows, tn), lambda l, j: (l, 0, j)),
        out_shape=jax.ShapeDtypeStruct((depth, rows, n), F32),
        compiler_params=_params("parallel", "parallel"),
        name="adaln",
    )(cvec, ada_w, ada_b.reshape(depth, 1, n))


def _inproj_a_kernel(x_ref, g_ref, ml_ref, mc_ref, wq_ref, wk_ref, wv_ref, wo_ref, wgi_ref, wgf_ref,
                     q_ref, k_ref, v_ref, o_ref, gi_ref, gf_ref):
    for s in range(x_ref.shape[0] // CHUNK):
        rows = slice(s * CHUNK, (s + 1) * CHUNK)
        mod = _tile_mod(ml_ref, mc_ref, pl.program_id(1), pl.num_programs(1), x_ref.shape[0], s)
        h = _rms_mod(x_ref[rows, :], g_ref[...], mod[0:1, :], mod[1:2, :]).astype(BF16)
        q_ref[rows, :] = _dot(h, wq_ref[...]).astype(BF16)
        k_ref[rows, :] = _dot(h, wk_ref[...]).astype(BF16)
        v_ref[rows, :] = _dot(h, wv_ref[...]).astype(BF16)
        o_ref[rows, :] = _dot(h, wo_ref[...]).astype(BF16)
        gi_ref[rows, :] = _dot(h, wgi_ref[...])
        gf_ref[rows, :] = _dot(h, wgf_ref[...])


def _inproj_b_kernel(x_ref, g_ref, ml_ref, mc_ref, cs_ref, sn_ref, cst_ref, snt_ref,
                     wq_ref, wkt_ref, wv_ref, wg_ref, q_ref, kt_ref, v_ref, gate_ref):
    lane = lax.broadcasted_iota(I32, (CHUNK, B_DK), 1)
    first = (lane % 64) < 32
    for s in range(x_ref.shape[0] // CHUNK):
        rows = slice(s * CHUNK, (s + 1) * CHUNK)
        mod = _tile_mod(ml_ref, mc_ref, pl.program_id(1), pl.num_programs(1), x_ref.shape[0], s)
        h = _rms_mod(x_ref[rows, :], g_ref[...], mod[0:1, :], mod[1:2, :]).astype(BF16)
        v_ref[rows, :] = _dot(h, wv_ref[...]).astype(BF16)
        gate_ref[rows, :] = _dot(h, wg_ref[...]).astype(BF16)
        cs, sn = cs_ref[rows, :], sn_ref[rows, :]
        q = _dot(h, wq_ref[...])
        for hd in range(B_HEADS):
            qh = q[:, hd * B_DK:(hd + 1) * B_DK]
            partner = jnp.where(first, pltpu.roll(qh, B_DK - 32, axis=1), pltpu.roll(qh, 32, axis=1))
            q_ref[rows, hd * B_DK:(hd + 1) * B_DK] = (qh * cs + partner * sn).astype(BF16)
        kt = _nt_dot(wkt_ref[...], h) * (B_DK ** -0.5)
        cst, snt = cst_ref[:, rows], snt_ref[:, rows]
        for hd in range(B_HEADS):
            for half in range(2):
                base = hd * B_DK + half * 64
                x1 = kt[base:base + 32, :]
                x2 = kt[base + 32:base + 64, :]
                c_ = cst[half * 32:(half + 1) * 32, :]
                s_ = snt[half * 32:(half + 1) * 32, :]
                kt_ref[base:base + 32, rows] = (x1 * c_ - x2 * s_).astype(BF16)
                kt_ref[base + 32:base + 64, rows] = (x1 * s_ + x2 * c_).astype(BF16)


def _tok_spec(cols, tm):
    return pl.BlockSpec((None, tm, cols), lambda b, t: (b, t, 0))


def _full_spec(shape):
    nd = len(shape)
    return pl.BlockSpec(shape, lambda b, t: (0,) * nd, pipeline_mode=pl.Buffered(1))


def _mod_specs(d):
    return [pl.BlockSpec((None, N_MOD, d), lambda b, t: (b, 0, 0)), _full_spec((N_MOD, d))]


def _inproj_a(xall, g, mod_l, mod_c, w_in):
    bn, tt, d = xall.shape
    tm = TOK_TILE
    nq, nv = A_HEADS * A_DK, A_HEADS * A_DV
    wq = w_in[:, :nq].astype(BF16)
    wk = w_in[:, nq:2 * nq].astype(BF16)
    wv = w_in[:, 2 * nq:2 * nq + nv].astype(BF16)
    wo = w_in[:, 2 * nq + nv:2 * nq + 2 * nv].astype(BF16)
    wg = w_in[:, 2 * nq + 2 * nv:]
    pad = jnp.zeros((d, 128 - 2 * A_HEADS), F32)
    perm = jnp.array(_GATE_PERM)
    wgi = jnp.concatenate([wg[:, :2 * A_HEADS][:, perm], pad], axis=1).astype(BF16)
    wgf = jnp.concatenate([wg[:, 2 * A_HEADS:][:, perm], pad], axis=1).astype(BF16)
    ws = [wq, wk, wv, wo, wgi, wgf]
    return pl.pallas_call(
        _inproj_a_kernel,
        grid=(bn, tt // tm),
        in_specs=[_tok_spec(d, tm), _full_spec((1, d))] + _mod_specs(d) + [_full_spec(w.shape) for w in ws],
        out_specs=[_tok_spec(nq, tm), _tok_spec(nq, tm),
                   _tok_spec(nv, tm), _tok_spec(nv, tm), _tok_spec(128, tm), _tok_spec(128, tm)],
        out_shape=[jax.ShapeDtypeStruct((bn, tt, nq), BF16), jax.ShapeDtypeStruct((bn, tt, nq), BF16),
                   jax.ShapeDtypeStruct((bn, tt, nv), BF16), jax.ShapeDtypeStruct((bn, tt, nv), BF16),
                   jax.ShapeDtypeStruct((bn, tt, 128), F32), jax.ShapeDtypeStruct((bn, tt, 128), F32)],
        compiler_params=_params("parallel", "parallel"),
        name="inproj_mlstm",
    )(xall, g.reshape(1, d), mod_l, mod_c, *ws)


def _rope_tables(tt):
    seq = tt - CHUNK
    rows = seq // GRID_W
    r = jnp.broadcast_to(jnp.arange(rows, dtype=F32)[:, None], (rows, GRID_W)).reshape(-1)
    col = jnp.broadcast_to(jnp.arange(GRID_W, dtype=F32)[None, :], (rows, GRID_W)).reshape(-1)
    nf = B_DK // 4
    inv = ROPE_BASE ** (-jnp.arange(nf, dtype=F32) / nf)
    ang = jnp.concatenate([r[:, None] * inv, col[:, None] * inv], axis=1)
    cos = jnp.concatenate([jnp.cos(ang), jnp.ones((CHUNK, 2 * nf), F32)], axis=0)
    sin = jnp.concatenate([jnp.sin(ang), jnp.zeros((CHUNK, 2 * nf), F32)], axis=0)
    cr, cc, sr, sc = cos[:, :nf], cos[:, nf:], sin[:, :nf], sin[:, nf:]
    cs = jnp.concatenate([cr, cr, cc, cc], axis=1)
    sn = jnp.concatenate([-sr, sr, -sc, sc], axis=1)
    return cs, sn, cos.T, sin.T


def _inproj_b(xall, g, mod_l, mod_c, w_in):
    bn, tt, d = xall.shape
    tm = TOK_TILE
    n = B_HEADS * B_DK
    wq = w_in[:, :n].astype(BF16)
    wkt = w_in[:, n:2 * n].T.astype(BF16)
    wv = w_in[:, 2 * n:3 * n].astype(BF16)
    wg = w_in[:, 3 * n:].astype(BF16)
    cs, sn, cst, snt = _rope_tables(tt)
    ws = [wq, wkt, wv, wg]
    tab_specs = [pl.BlockSpec((tm, B_DK), lambda b, t: (t, 0)), pl.BlockSpec((tm, B_DK), lambda b, t: (t, 0)),
                 pl.BlockSpec((64, tm), lambda b, t: (0, t)), pl.BlockSpec((64, tm), lambda b, t: (0, t))]
    return pl.pallas_call(
        _inproj_b_kernel,
        grid=(bn, tt // tm),
        in_specs=[_tok_spec(d, tm), _full_spec((1, d))] + _mod_specs(d) + tab_specs
                 + [_full_spec(w.shape) for w in ws],
        out_specs=[_tok_spec(n, tm), pl.BlockSpec((None, n, tm), lambda b, t: (b, 0, t)),
                   _tok_spec(n, tm), _tok_spec(n, tm)],
        out_shape=[jax.ShapeDtypeStruct((bn, tt, n), BF16), jax.ShapeDtypeStruct((bn, n, tt), BF16),
                   jax.ShapeDtypeStruct((bn, tt, n), BF16), jax.ShapeDtypeStruct((bn, tt, n), BF16)],
        compiler_params=_params("parallel", "parallel"),
        name="inproj_ret",
    )(xall, g.reshape(1, d), mod_l, mod_c, cs, sn, cst, snt, *ws)


_GATE_PERM = [d * A_HEADS + 2 * hp + hh for hp in range(A_HEADS // 2) for d in range(2) for hh in range(2)]


def _scan_chunks(tt, chunk, reverse):
    n = tt // chunk
    nctx = CHUNK // chunk
    order = list(range(n - nctx, n)) + list(range(n - nctx))
    if reverse:
        order = list(range(n - 1, n - nctx - 1, -1)) + list(range(n - nctx - 1, -1, -1))
    return order


def _chunk_scan(x, pos, op, reverse):
    n = x.shape[0]
    s = 1
    while s < CHUNK:
        if reverse:
            sh = pltpu.roll(x, n - s, axis=0)
            ok = pos < CHUNK - s
        else:
            sh = pltpu.roll(x, s, axis=0)
            ok = pos >= s
        x = jnp.where(ok, op(x, sh), x)
        s *= 2
    return x


def _mlstm_gates_kernel(gi_ref, gf_ref, bi_ref, bf_ref, b_ref, m_ref, at_ref):
    ig = gi_ref[...] + bi_ref[...]
    lf = _log_sigmoid(gf_ref[...] + bf_ref[...])
    tt = ig.shape[0]
    pos = lax.broadcasted_iota(I32, (tt, 1), 0) % CHUNK
    fwd = (lax.broadcasted_iota(I32, (1, ig.shape[1]), 1) // 2) % 2 == 0
    pre = _chunk_scan(lf, pos, jnp.add, False)
    suf = jnp.concatenate([pre[c * CHUNK + CHUNK - 1:(c + 1) * CHUNK, :] - pre[c * CHUNK:(c + 1) * CHUNK, :]
                           for c in range(tt // CHUNK)], axis=0) + lf
    b = jnp.where(fwd, pre, suf)
    a = ig - b
    mloc = jnp.where(fwd, _chunk_scan(a, pos, jnp.maximum, False), _chunk_scan(a, pos, jnp.maximum, True))
    for hp in range(A_HEADS // 2):
        b_ref[hp] = b[:, 4 * hp:4 * hp + 4]
        m_ref[hp] = mloc[:, 4 * hp:4 * hp + 4]
    at_ref[...] = a.T[:2 * A_HEADS, :]


def _mlstm_gates(gi, gf, ig_b, fg_b):
    bn, tt, w = gi.shape
    nc = 2 * A_HEADS
    perm = jnp.array(_GATE_PERM)
    padrow = lambda v: jnp.concatenate([v.reshape(nc)[perm].reshape(1, nc).astype(F32),
                                        jnp.zeros((1, w - nc), F32)], axis=1)
    nhp = A_HEADS // 2
    spec = pl.BlockSpec((None, tt, w), lambda b: (b, 0, 0))
    row = pl.BlockSpec((1, w), lambda b: (0, 0))
    col_spec = pl.BlockSpec((None, nhp, tt, 4), lambda b: (b, 0, 0, 0))
    row_spec = pl.BlockSpec((None, nc, tt), lambda b: (b, 0, 0))
    col_shape = jax.ShapeDtypeStruct((bn, nhp, tt, 4), F32)
    row_shape = jax.ShapeDtypeStruct((bn, nc, tt), F32)
    return pl.pallas_call(
        _mlstm_gates_kernel,
        grid=(bn,),
        in_specs=[spec, spec, row, row],
        out_specs=[col_spec, col_spec, row_spec],
        out_shape=[col_shape, col_shape, row_shape],
        compiler_params=_params("parallel"),
        name="mlstm_gates",
    )(gi, gf, padrow(ig_b), padrow(fg_b))


CONV_PAD = 8


def _mlstm_scan_kernel(q_ref, k_ref, v_ref, o_ref, b_ref, m_ref, at_ref, cwq_ref, cbq_ref, cwk_ref, cbk_ref,
                       ng_ref, y_ref, xq_ref, xk_ref, qs_ref, ks_ref, hs_ref):
    tt = q_ref.shape[0]
    seq = tt - CHUNK
    half = CONV_W // 2
    zeros = jnp.zeros((CONV_PAD, q_ref.shape[1]), F32)
    bases = []
    for c in range(tt // CHUNK):
        r0 = c * CHUNK
        base = CONV_PAD + r0 + (2 * CONV_PAD if r0 >= seq else 0)
        bases.append(base)
        xq_ref[base:base + CHUNK, :] = q_ref[r0:r0 + CHUNK, :].astype(F32)
        xk_ref[base:base + CHUNK, :] = k_ref[r0:r0 + CHUNK, :].astype(F32)
    for x_ref in (xq_ref, xk_ref):
        x_ref[0:CONV_PAD, :] = zeros
        x_ref[CONV_PAD + seq:3 * CONV_PAD + seq, :] = jnp.concatenate([zeros, zeros], axis=0)
        x_ref[3 * CONV_PAD + tt:4 * CONV_PAD + tt, :] = zeros
    for c, base in enumerate(bases):
        rows = slice(c * CHUNK, (c + 1) * CHUNK)
        qa = cbq_ref[...]
        ka = cbk_ref[...]
        for w in range(CONV_W):
            qa = qa + xq_ref[base + w - half:base + w - half + CHUNK, :] * cwq_ref[w:w + 1, :]
            ka = ka + xk_ref[base + w - half:base + w - half + CHUNK, :] * cwk_ref[w:w + 1, :]
        qs_ref[rows, :] = (qa * _sigmoid(qa) * (A_DK ** -0.5)).astype(BF16)
        ks_ref[:, rows] = (ka * _sigmoid(ka)).T.astype(BF16)

    ii = lax.broadcasted_iota(I32, (CHUNK, CHUNK), 0)
    jj = lax.broadcasted_iota(I32, (CHUNK, CHUNK), 1)
    ones_blk = jnp.ones((CHUNK, A_DV), BF16)

    for hh in range(2):
        for d in range(2):
            mask = (jj >= ii) if d else (jj <= ii)
            last = 0 if d else CHUNK - 1
            state = jnp.zeros((A_DK, 2 * A_DV), F32)
            m = jnp.zeros((1, 1), F32)
            for c in _scan_chunks(tt, CHUNK, bool(d)):
                r0 = c * CHUNK
                qc = qs_ref[r0:r0 + CHUNK, hh * A_DK:(hh + 1) * A_DK]
                ktc = ks_ref[hh * A_DK:(hh + 1) * A_DK, r0:r0 + CHUNK]
                vaug = jnp.concatenate([v_ref[r0:r0 + CHUNK, hh * A_DV:(hh + 1) * A_DV], ones_blk], axis=1)
                col = slice(2 * d + hh, 2 * d + hh + 1)
                a_row = at_ref[col, r0:r0 + CHUNK]
                mcol = jnp.maximum(m, m_ref[r0:r0 + CHUNK, col])
                bcol = b_ref[r0:r0 + CHUNK, col]
                decay = jnp.where(mask, jnp.exp(a_row - mcol), 0.0)
                s = (_dot(qc, ktc) * decay).astype(BF16)
                qw = (qc.astype(F32) * jnp.exp(m - mcol)).astype(BF16)
                nd = _dot(jnp.concatenate([s, qw], axis=1), jnp.concatenate([vaug, state.astype(BF16)], axis=0))
                hout = nd[:, :A_DV] / jnp.maximum(jnp.abs(nd[:, A_DV:]), jnp.exp(-(bcol + mcol)))
                dst = (slice(r0, r0 + CHUNK), slice(hh * A_DV, (hh + 1) * A_DV))
                if d:
                    hs_ref[dst] = hs_ref[dst] + hout
                else:
                    hs_ref[dst] = hout
                m_last = mcol[last:last + 1, :]
                wk = jnp.exp(a_row - m_last)
                state = jnp.exp(m - m_last) * state + _dot((ktc.astype(F32) * wk).astype(BF16), vaug)
                m = bcol[last:last + 1, :] + m_last
    for c in range(tt // CHUNK):
        rows = slice(c * CHUNK, (c + 1) * CHUNK)
        for hh in range(2):
            sl = slice(hh * A_DV, (hh + 1) * A_DV)
            y = hs_ref[rows, sl]
            y = y * lax.rsqrt(jnp.mean(y * y, axis=-1, keepdims=True) + EPS)
            y = y * ng_ref[:, sl] * _sigmoid(o_ref[rows, sl].astype(F32))
            y_ref[rows, sl] = y.astype(BF16)


def _mlstm_scan(q, k, v, o, bcs, mloc, at, conv_w, conv_b, norm_g):
    bn, tt, nq = q.shape
    nv = v.shape[-1]
    nhp = A_HEADS // 2
    col_spec = pl.BlockSpec((None, None, tt, 4), lambda b, h: (b, h, 0, 0))
    row_spec = pl.BlockSpec((None, None, 4, tt), lambda b, h: (b, h, 0, 0))
    cwq = conv_w[:, :nq]
    cwk = conv_w[:, nq:]
    cbq = conv_b[:nq].reshape(1, nq)
    cbk = conv_b[nq:].reshape(1, nq)
    return pl.pallas_call(
        _mlstm_scan_kernel,
        grid=(bn, A_HEADS // 2),
        in_specs=[pl.BlockSpec((None, tt, 2 * A_DK), lambda b, h: (b, 0, h)),
                  pl.BlockSpec((None, tt, 2 * A_DK), lambda b, h: (b, 0, h)),
                  pl.BlockSpec((None, tt, 2 * A_DV), lambda b, h: (b, 0, h)),
                  pl.BlockSpec((None, tt, 2 * A_DV), lambda b, h: (b, 0, h))]
                 + [col_spec, col_spec, row_spec] + [
                  pl.BlockSpec((CONV_W, 2 * A_DK), lambda b, h: (0, h)),
                  pl.BlockSpec((1, 2 * A_DK), lambda b, h: (0, h)),
                  pl.BlockSpec((CONV_W, 2 * A_DK), lambda b, h: (0, h)),
                  pl.BlockSpec((1, 2 * A_DK), lambda b, h: (0, h)),
                  pl.BlockSpec((1, 2 * A_DV), lambda b, h: (0, h))],
        out_specs=pl.BlockSpec((None, tt, 2 * A_DV), lambda b, h: (b, 0, h)),
        out_shape=jax.ShapeDtypeStruct((bn, tt, nv), BF16),
        scratch_shapes=[pltpu.VMEM((tt + 4 * CONV_PAD, 2 * A_DK), F32), pltpu.VMEM((tt + 4 * CONV_PAD, 2 * A_DK), F32),
                        pltpu.VMEM((tt, 2 * A_DK), BF16), pltpu.VMEM((2 * A_DK, tt), BF16),
                        pltpu.VMEM((tt, 2 * A_DV), F32)],
        compiler_params=_params("parallel", "parallel"),
        name="mlstm_scan",
    )(q, k, v, o, bcs, mloc, at.reshape(bn, nhp, 4, tt), cwq, cbq, cwk, cbk, norm_g.reshape(1, nv))


RET_HEADS = 2


def _ret_scan_kernel(q_ref, kt_ref, v_ref, g_ref, dl_ref, gn_ref, y_ref, st_ref):
    tt = q_ref.shape[0]
    L = CHUNK
    ii = lax.broadcasted_iota(I32, (L, L), 0)
    jj = lax.broadcasted_iota(I32, (L, L), 1)
    ci = lax.broadcasted_iota(I32, (L, 1), 0).astype(F32)
    rj = lax.broadcasted_iota(I32, (1, L), 1).astype(F32)
    dist = (ii - jj).astype(F32)
    for hh in range(RET_HEADS):
        cols = slice(hh * B_DK, (hh + 1) * B_DK)
        logg = [_log_sigmoid(dl_ref[0, hh]), _log_sigmoid(dl_ref[1, hh])]
        for d in range(2):
            order = _scan_chunks(tt, L, bool(d))
            kdec = jnp.exp((rj if d else L - 1.0 - rj) * logg[d])
            sdec = jnp.exp(L * logg[d])
            state = jnp.zeros((B_DK, B_DV), F32)
            for n, c in enumerate(order):
                rows = slice(c * L, (c + 1) * L)
                st_ref[hh, d, c] = state.astype(BF16)
                if n + 1 < len(order):
                    state = sdec * state + _dot((kt_ref[cols, rows].astype(F32) * kdec).astype(BF16), v_ref[rows, cols])
        dcomb = (jnp.where(ii >= jj, jnp.exp(dist * logg[0]), 0.0)
                 + jnp.where(jj >= ii, jnp.exp(-dist * logg[1]), 0.0))
        inter_f = jnp.exp((ci + 1.0) * logg[0])
        inter_b = jnp.exp((L - ci) * logg[1])
        for c in range(tt // L):
            rows = slice(c * L, (c + 1) * L)
            qc = q_ref[rows, cols]
            q32 = qc.astype(F32)
            s = (_dot(qc, kt_ref[cols, rows]) * dcomb).astype(BF16)
            lhs = jnp.concatenate([s, (q32 * inter_f).astype(BF16), (q32 * inter_b).astype(BF16)], axis=1)
            rhs = jnp.concatenate([v_ref[rows, cols], st_ref[hh, 0, c], st_ref[hh, 1, c]], axis=0)
            y = _dot(lhs, rhs)
            mu = jnp.mean(y, axis=-1, keepdims=True)
            yc = y - mu
            var = jnp.mean(yc * yc, axis=-1, keepdims=True)
            g = g_ref[rows, cols].astype(F32)
            y_ref[rows, cols] = (yc * lax.rsqrt(var + EPS) * gn_ref[:, cols] * (g * _sigmoid(g))).astype(BF16)


def _ret_scan(q, kt, v, g, decay_logit, gn_g):
    bn, tt, n = q.shape
    nh = RET_HEADS
    dl = decay_logit.astype(F32).reshape(2, B_HEADS // nh, nh, 1, 1).transpose(1, 0, 2, 3, 4)
    tok = pl.BlockSpec((None, tt, nh * B_DK), lambda b, h: (b, 0, h))
    return pl.pallas_call(
        _ret_scan_kernel,
        grid=(bn, B_HEADS // nh),
        in_specs=[tok, pl.BlockSpec((None, nh * B_DK, tt), lambda b, h: (b, h, 0)), tok, tok,
                  pl.BlockSpec((None, 2, nh, 1, 1), lambda b, h: (h, 0, 0, 0, 0)),
                  pl.BlockSpec((1, nh * B_DV), lambda b, h: (0, h))],
        out_specs=tok,
        out_shape=jax.ShapeDtypeStruct((bn, tt, n), BF16),
        scratch_shapes=[pltpu.VMEM((nh, 2, tt // CHUNK, B_DK, B_DV), BF16)],
        compiler_params=_params("parallel", "parallel"),
        name="ret_scan",
    )(q, kt, v, g, dl, gn_g.reshape(1, n))


def _outproj_kernel(y_ref, x_ref, w_ref, ml_ref, mc_ref, g_ref, r2_ref, x1_ref, h2_ref, lg_ref):
    for s in range(x_ref.shape[0] // CHUNK):
        rows = slice(s * CHUNK, (s + 1) * CHUNK)
        mod = _tile_mod(ml_ref, mc_ref, pl.program_id(1), pl.num_programs(1), x_ref.shape[0], s)
        out = _dot(y_ref[rows, :], w_ref[...])
        x1 = x_ref[rows, :] + mod[2:3, :] * out
        x1_ref[rows, :] = x1
        h2 = _rms_mod(x1, g_ref[...], mod[3:4, :], mod[4:5, :])
        hi = h2.astype(BF16)
        h2_ref[rows, :] = hi
        lo = (h2 - hi.astype(F32)).astype(BF16)
        a = _dot(hi, r2_ref[...])
        lg_ref[rows, :] = (a[:, :LANES] + a[:, LANES:]) + _dot(lo, r2_ref[:, :LANES])


def _outproj(y, xall, w_out, mod_l, mod_c, g, router):
    bn, tt, d = xall.shape
    tm = TOK_TILE
    ne = router.shape[1]
    r = jnp.concatenate([router.astype(F32), jnp.zeros((d, LANES - ne), F32)], axis=1)
    rhi = r.astype(BF16)
    r2 = jnp.concatenate([rhi, (r - rhi.astype(F32)).astype(BF16)], axis=1)
    return pl.pallas_call(
        _outproj_kernel,
        grid=(bn, tt // tm),
        in_specs=[_tok_spec(y.shape[-1], tm), _tok_spec(d, tm), _full_spec(w_out.shape)] + _mod_specs(d)
                 + [_full_spec((1, d)), _full_spec((d, 2 * LANES))],
        out_specs=[_tok_spec(d, tm), _tok_spec(d, tm), _tok_spec(LANES, tm)],
        out_shape=[jax.ShapeDtypeStruct((bn, tt, d), F32), jax.ShapeDtypeStruct((bn, tt, d), BF16),
                   jax.ShapeDtypeStruct((bn, tt, LANES), F32)],
        compiler_params=_params("parallel", "parallel"),
        name="outproj_router",
    )(y, xall, w_out.astype(BF16), mod_l, mod_c, g.reshape(1, d), r2)


def _lane_cumsum(x, upper):
    n = x.shape[1]
    parts = []
    carry = jnp.zeros((x.shape[0], 1), F32)
    for j in range(n // CHUNK):
        cs = _dot(x[:, j * CHUNK:(j + 1) * CHUNK].astype(BF16), upper) + carry
        parts.append(cs)
        carry = cs[:, CHUNK - 1:CHUNK]
    return parts[0] if len(parts) == 1 else jnp.concatenate(parts, axis=1)


def _kth_largest(vals, caps):
    def count(x, cand_bits):
        return jnp.sum((x >= pltpu.bitcast(cand_bits, F32)).astype(F32), axis=1, keepdims=True)

    taus = [jnp.zeros((x.shape[0], 1), I32) for x in vals]
    for p, nbits in [(30, 1)] + [(p, 2) for p in range(28, -1, -2)]:
        for i, (x, cap) in enumerate(zip(vals, caps)):
            tau = taus[i]
            digit = sum((count(x, tau + (j << p)) >= cap).astype(I32) for j in range(1, 1 << nbits))
            taus[i] = tau + digit * (1 << p)
    return [pltpu.bitcast(tau, F32) for tau in taus]


ROUTE_BATCH = 2


def _route_kernel(lg_ref, pos_ref, post_ref, gatet_ref, gate_ref, *, segs):
    nb, ne = pos_ref.shape[0], pos_ref.shape[1]
    si = lax.broadcasted_iota(I32, (CHUNK, CHUNK), 0)
    ti = lax.broadcasted_iota(I32, (CHUNK, CHUNK), 1)
    upper = (si <= ti).astype(BF16)
    work = []
    for i in range(nb):
        lgt = lg_ref[i].T[:ne, :]
        for seg in segs:
            s0, n, cap, tag = seg
            lg = lgt[:, s0:s0 + n]
            e = jnp.exp(lg - jnp.max(lg, axis=0, keepdims=True))
            work.append((i, seg, e / jnp.sum(e, axis=0, keepdims=True)))
    taus = _kth_largest([aff for (_, _, aff) in work], [seg[2] for (_, seg, _) in work])
    for (i, (s0, n, cap, tag), aff), tau in zip(work, taus):
        gt = aff > tau
        eq = aff == tau
        need = cap - jnp.sum(gt.astype(F32), axis=1, keepdims=True)
        cum_eq = _lane_cumsum(eq.astype(F32), upper)
        sel = gt | (eq & (cum_eq <= need))
        pos = _lane_cumsum(sel.astype(F32), upper) - 1.0 + tag
        pos_ref[i, :, s0:s0 + n] = jnp.where(sel, pos, -1.0)
        gate_ref[i, :, s0:s0 + n] = aff
    pad = jnp.zeros((LANES - ne, pos_ref.shape[2]), F32)
    for i in range(nb):
        post_ref[i] = jnp.concatenate([pos_ref[i], pad], axis=0).T[:, :ne]
        gatet_ref[i] = jnp.concatenate([gate_ref[i], pad], axis=0).T[:, :ne]


def _route(logits, ne, segs):
    bn, tt, w = logits.shape
    nb = ROUTE_BATCH
    row = pl.BlockSpec((nb, ne, tt), lambda b: (b, 0, 0))
    col = pl.BlockSpec((nb, tt, ne), lambda b: (b, 0, 0))
    return pl.pallas_call(
        functools.partial(_route_kernel, segs=segs),
        grid=(bn // nb,),
        in_specs=[pl.BlockSpec((nb, tt, w), lambda b: (b, 0, 0))],
        out_specs=[row, col, col],
        out_shape=[jax.ShapeDtypeStruct((bn, ne, tt), F32),
                   jax.ShapeDtypeStruct((bn, tt, ne), F32), jax.ShapeDtypeStruct((bn, tt, ne), F32)],
        scratch_shapes=[pltpu.VMEM((nb, ne, tt), F32)],
        compiler_params=_params("parallel"),
        name="route",
    )(logits)


EXP_GROUP = 4


def _onehot_rows(pos_ref, lo, hi, cap, tag):
    parts = []
    slot = lax.broadcasted_iota(I32, (cap, hi - lo), 0).astype(F32) + tag
    for r in range(EXP_GROUP):
        parts.append(jnp.where(pos_ref[r:r + 1, lo:hi] == slot, 1.0, 0.0).astype(BF16))
    return jnp.concatenate(parts, axis=0)


def _gather_kernel(pos_ref, h_ref, xl_ref, *maybe_xc_ref, cap_c, cap_l):
    tt = h_ref.shape[0]
    seq = tt - CHUNK
    xl_ref[...] = _dot(_onehot_rows(pos_ref, 0, seq, cap_l, 0.0), h_ref[0:seq, :]).astype(BF16)
    for xc_ref in maybe_xc_ref:
        xc_ref[...] = _dot(_onehot_rows(pos_ref, seq, tt, cap_c, float(cap_l)), h_ref[seq:tt, :]).astype(BF16)


def _gather(pos, h2, cap_c, cap_l, with_ctx):
    bn, ne, tt = pos.shape
    d = h2.shape[-1]
    ng = ne // EXP_GROUP
    pos4 = pos.reshape(bn, ng, EXP_GROUP, tt)
    caps = [cap_l] + ([cap_c] if with_ctx else [])
    return pl.pallas_call(
        functools.partial(_gather_kernel, cap_c=cap_c, cap_l=cap_l),
        grid=(bn, ng),
        in_specs=[pl.BlockSpec((None, None, EXP_GROUP, tt), lambda b, g: (b, g, 0, 0)),
                  pl.BlockSpec((None, tt, d), lambda b, g: (b, 0, 0))],
        out_specs=[pl.BlockSpec((None, EXP_GROUP * cap, d), lambda b, g: (b, g, 0)) for cap in caps],
        out_shape=[jax.ShapeDtypeStruct((bn, ne * cap, d), BF16) for cap in caps],
        compiler_params=_params("parallel", "arbitrary"),
        name="moe_gather",
    )(pos4, h2)


FFN_BATCH = 4


def _ffn_kernel(*refs, nseg):
    x_refs = refs[:nseg]
    wg_ref, wu_ref, wd_ref = refs[nseg:nseg + 3]
    y_refs = refs[nseg + 3:2 * nseg + 3]
    wg16, wu16, wd16 = refs[2 * nseg + 3:]

    @pl.when(pl.program_id(1) == 0)
    def _():
        wg16[...] = wg_ref[...].astype(BF16)
        wu16[...] = wu_ref[...].astype(BF16)
        wd16[...] = wd_ref[...].astype(BF16)

    d = x_refs[0].shape[-1]
    rows = [r.shape[0] * r.shape[1] for r in x_refs]
    xs = [r[...].reshape(n, d) for r, n in zip(x_refs, rows)]
    x = xs[0] if nseg == 1 else jnp.concatenate(xs, axis=0)
    g = _dot(x, wg16[...])
    u = _dot(x, wu16[...])
    hid = (g * _sigmoid(g) * u).astype(BF16)
    y = _dot(hid, wd16[...]).astype(BF16)
    r0 = 0
    for y_ref, n in zip(y_refs, rows):
        y_ref[...] = y[r0:r0 + n].reshape(y_ref.shape)
        r0 += n


def _ffn(xs, caps, w_gate, w_up, w_down, layer):
    bn = xs[0].shape[0]
    _, ne, d, f = w_gate.shape
    bb = FFN_BATCH
    x4 = [x.reshape(bn, ne, cap, d) for x, cap in zip(xs, caps)]
    specs = [pl.BlockSpec((bb, None, cap, d), lambda e, b: (b, e, 0, 0)) for cap in caps]
    wspec = lambda s: pl.BlockSpec((None, None) + s, lambda e, b: (layer, e, 0, 0))
    ys = pl.pallas_call(
        functools.partial(_ffn_kernel, nseg=len(xs)),
        grid=(ne, bn // bb),
        in_specs=specs + [wspec((d, f)), wspec((d, f)), wspec((f, d))],
        out_specs=specs,
        out_shape=[jax.ShapeDtypeStruct(x.shape, BF16) for x in x4],
        scratch_shapes=[pltpu.VMEM((d, f), BF16), pltpu.VMEM((d, f), BF16), pltpu.VMEM((f, d), BF16)],
        compiler_params=_params("parallel", "arbitrary"),
        name="moe_ffn",
    )(*x4, w_gate, w_up, w_down)
    return [y.reshape(x.shape) for y, x in zip(ys, xs)]


SCATTER_TILE = 512


def _onehot_cols(post_ref, gatet_ref, cap, tag):
    nrows, ne = post_ref.shape
    slot = lax.broadcasted_iota(I32, (nrows, cap), 1).astype(F32) + tag
    parts = [jnp.where(post_ref[:, e:e + 1] == slot, gatet_ref[:, e:e + 1], 0.0).astype(BF16) for e in range(ne)]
    return jnp.concatenate(parts, axis=1)


def _scatter_lat_kernel(post_ref, gatet_ref, yl_ref, x_ref, ml_ref, fg_ref, o_ref, *, cap, final):
    moe = _dot(_onehot_cols(post_ref, gatet_ref, cap, 0.0), yl_ref[...])
    x2 = x_ref[...] + ml_ref[5:6, :] * moe
    if final:
        x2 = x2 * lax.rsqrt(jnp.mean(x2 * x2, axis=-1, keepdims=True) + EPS) * fg_ref[...]
    o_ref[...] = x2


def _scatter_ctx_kernel(post_ref, gatet_ref, yc_ref, x_ref, mc_ref, o_ref, *, cap, tag):
    moe = _dot(_onehot_cols(post_ref, gatet_ref, cap, tag), yc_ref[...])
    o_ref[...] = x_ref[...] + mc_ref[5:6, :] * moe


def _scatter_lat(post, gatet, yl, x1, mod_l, final_g, cap, final):
    bn, tt, d = x1.shape
    ne = post.shape[-1]
    seq = tt - CHUNK
    tm = SCATTER_TILE
    tok = lambda cols: pl.BlockSpec((None, tm, cols), lambda b, t: (b, t, 0))
    return pl.pallas_call(
        functools.partial(_scatter_lat_kernel, cap=cap, final=final),
        grid=(bn, seq // tm),
        in_specs=[tok(ne), tok(ne), pl.BlockSpec((None,) + yl.shape[1:], lambda b, t: (b, 0, 0)), tok(d),
                  pl.BlockSpec((None, N_MOD, d), lambda b, t: (b, 0, 0)), _full_spec((1, d))],
        out_specs=tok(d),
        out_shape=jax.ShapeDtypeStruct((bn, seq if final else tt, d), F32),
        input_output_aliases={} if final else {3: 0},
        compiler_params=_params("parallel", "arbitrary"),
        name="moe_scatter_final" if final else "moe_scatter_lat",
    )(post, gatet, yl, x1, mod_l, final_g.reshape(1, d))


def _scatter_ctx(post, gatet, yc, x1, mod_c, cap, tag):
    bn, tt, d = x1.shape
    ne = post.shape[-1]
    cblk = tt // CHUNK - 1
    tok = lambda cols: pl.BlockSpec((None, CHUNK, cols), lambda b: (b, cblk, 0))
    return pl.pallas_call(
        functools.partial(_scatter_ctx_kernel, cap=cap, tag=tag),
        grid=(bn,),
        in_specs=[tok(ne), tok(ne), pl.BlockSpec((None,) + yc.shape[1:], lambda b: (b, 0, 0)), tok(d),
                  pl.BlockSpec((N_MOD, d), lambda b: (0, 0))],
        out_specs=tok(d),
        out_shape=jax.ShapeDtypeStruct((bn, tt, d), F32),
        input_output_aliases={3: 0},
        compiler_params=_params("parallel"),
        name="moe_scatter_ctx",
    )(post, gatet, yc, x1, mod_c)


def kernel(x, c, ctx, c_ctx, ada_w, ada_b, norm_mix_g, norm_ffn_g, final_norm_g, mlstm_w_in, mlstm_conv_w,
           mlstm_conv_b, mlstm_igate_b, mlstm_fgate_b, mlstm_head_norm_g, mlstm_w_out, ret_w_in, ret_decay_logit,
           ret_group_norm_g, ret_w_out, moe_router, moe_w_gate, moe_w_up, moe_w_down):
    bn, seq, d = x.shape
    depth = ada_w.shape[0]
    ne = moe_router.shape[-1]
    assert ctx.shape[1] == CHUNK and seq % SCATTER_TILE == 0 and (seq + CHUNK) % TOK_TILE == 0
    cap_c = CAPACITY_FACTOR * CHUNK // ne
    cap_l = CAPACITY_FACTOR * seq // ne
    assert cap_l % LANES == 0
    segs = ((0, seq, cap_l, 0.0), (seq, CHUNK, cap_c, float(cap_l)))

    rows = -(-(bn + 1) // 8) * 8
    cvec = jnp.concatenate([c, c_ctx[None, :], jnp.zeros((rows - bn - 1, d), F32)], axis=0)
    mod = _adaln(cvec, ada_w, ada_b)
    xall = jnp.concatenate([x, ctx], axis=1)

    for i in range(depth):
        last = i == depth - 1
        mod_l = mod[i, :bn].reshape(bn, N_MOD, d)
        mod_c = mod[i, bn].reshape(N_MOD, d)
        j = i // 2
        if i % 2 == 0:
            q, kt, v, o, gi, gf = _inproj_a(xall, norm_mix_g[i], mod_l, mod_c, mlstm_w_in[j])
            bcs, mloc, at = _mlstm_gates(gi, gf, mlstm_igate_b[j], mlstm_fgate_b[j])
            y = _mlstm_scan(q, kt, v, o, bcs, mloc, at, mlstm_conv_w[j], mlstm_conv_b[j], mlstm_head_norm_g[j])
            w_out = mlstm_w_out[j]
        else:
            q, kt, v, g = _inproj_b(xall, norm_mix_g[i], mod_l, mod_c, ret_w_in[j])
            y = _ret_scan(q, kt, v, g, ret_decay_logit[j], ret_group_norm_g[j])
            w_out = ret_w_out[j]
        x1, h2, logits = _outproj(y, xall, w_out, mod_l, mod_c, norm_ffn_g[i], moe_router[i])
        pos, post, gatet = _route(logits, ne, segs)
        caps = [cap_l] if last else [cap_l, cap_c]
        xs = _gather(pos, h2, cap_c, cap_l, not last)
        ys = _ffn(xs, caps, moe_w_gate, moe_w_up, moe_w_down, i)
        xall = _scatter_lat(post, gatet, ys[0], x1, mod_l, final_norm_g, cap_l, last)
        if not last:
            xall = _scatter_ctx(post, gatet, ys[1], xall, mod_c, cap_c, float(cap_l))
    return xall
```

```python
import functools

import jax
import jax.numpy as jnp
from jax import lax
from jax.experimental import pallas as pl
from jax.experimental.pallas import tpu as pltpu

F32 = jnp.float32
BF16 = jnp.bfloat16
I32 = jnp.int32

EPS = 1e-6
N_MOD = 6
CHUNK = 256
TOK_TILE = 768
A_HEADS, A_DK, A_DV = 8, 64, 128
B_HEADS, B_DK, B_DV = 8, 128, 128
CONV_W = 5
CAPACITY_FACTOR = 2
GRID_W = 64
ROPE_BASE = 10000.0
LANES = 128
VMEM_LIMIT = 56 * 1024 * 1024


def _params(*sem):
    return pltpu.CompilerParams(dimension_semantics=sem, vmem_limit_bytes=VMEM_LIMIT)


def _sigmoid(x):
    return 0.5 * jnp.tanh(0.5 * x) + 0.5


def _log_sigmoid(x):
    return jnp.minimum(x, 0.0) - jnp.log1p(jnp.exp(-jnp.abs(x)))


def _nt_dot(a, b, **kw):
    return lax.dot_general(a, b, (((1,), (1,)), ((), ())), preferred_element_type=F32, **kw)


def _dot(a, b, **kw):
    return jnp.dot(a, b, preferred_element_type=F32, **kw)


def _rms_mod(x, g, shift, scale):
    y = x * lax.rsqrt(jnp.mean(x * x, axis=-1, keepdims=True) + EPS)
    return (y * g) * (1.0 + scale) + shift


def _tile_mod(mod_l_ref, mod_c_ref, tile, ntiles, rows, sub):
    if (sub + 1) * CHUNK < rows:
        return mod_l_ref[...]
    return jnp.where(tile == ntiles - 1, mod_c_ref[...], mod_l_ref[...])


def _adaln_kernel(c_ref, w_ref, b_ref, o_ref):
    c = c_ref[...]
    s = c * _sigmoid(c)
    w = w_ref[...]
    s_hi = s.astype(BF16)
    w_hi = w.astype(BF16)
    s_lo = (s - s_hi.astype(F32)).astype(BF16)
    w_lo = (w - w_hi.astype(F32)).astype(BF16)
    o_ref[...] = _dot(s_hi, w_hi) + (_dot(s_lo, w_hi) + _dot(s_hi, w_lo)) + b_ref[...]


def _adaln(cvec, ada_w, ada_b):
    depth, d, n = ada_w.shape
    rows = cvec.shape[0]
    tn = 1536
    return pl.pallas_call(
        _adaln_kernel,
        grid=(depth, n // tn),
        in_specs=[pl.BlockSpec((rows, d), lambda l, j: (0, 0)),
                  pl.BlockSpec((None, d, tn), lambda l, j: (l, 0, j)),
                  pl.BlockSpec((None, 1, tn), lambda l, j: (l, 0, j))],
        out_specs=pl.BlockSpec((None, rows, tn), lambda l, j: (l, 0, j)),
        out_shape=jax.ShapeDtypeStruct((depth, rows, n), F32),
        compiler_params=_params("parallel", "parallel"),
        name="adaln",
    )(cvec, ada_w, ada_b.reshape(depth, 1, n))


def _inproj_a_kernel(x_ref, g_ref, ml_ref, mc_ref, wq_ref, wk_ref, wv_ref, wo_ref, wgi_ref, wgf_ref,
                     q_ref, k_ref, v_ref, o_ref, gi_ref, gf_ref):
    for s in range(x_ref.shape[0] // CHUNK):
        rows = slice(s * CHUNK, (s + 1) * CHUNK)
        mod = _tile_mod(ml_ref, mc_ref, pl.program_id(1), pl.num_programs(1), x_ref.shape[0], s)
        h = _rms_mod(x_ref[rows, :], g_ref[...], mod[0:1, :], mod[1:2, :]).astype(BF16)
        q_ref[rows, :] = _dot(h, wq_ref[...]).astype(BF16)
        k_ref[rows, :] = _dot(h, wk_ref[...]).astype(BF16)
        v_ref[rows, :] = _dot(h, wv_ref[...]).astype(BF16)
        o_ref[rows, :] = _dot(h, wo_ref[...]).astype(BF16)
        gi_ref[rows, :] = _dot(h, wgi_ref[...])
        gf_ref[rows, :] = _dot(h, wgf_ref[...])


def _inproj_b_kernel(x_ref, g_ref, ml_ref, mc_ref, cs_ref, sn_ref, cst_ref, snt_ref,
                     wq_ref, wkt_ref, wv_ref, wg_ref, q_ref, kt_ref, v_ref, gate_ref):
    lane = lax.broadcasted_iota(I32, (CHUNK, B_DK), 1)
    first = (lane % 64) < 32
    for s in range(x_ref.shape[0] // CHUNK):
        rows = slice(s * CHUNK, (s + 1) * CHUNK)
        mod = _tile_mod(ml_ref, mc_ref, pl.program_id(1), pl.num_programs(1), x_ref.shape[0], s)
        h = _rms_mod(x_ref[rows, :], g_ref[...], mod[0:1, :], mod[1:2, :]).astype(BF16)
        v_ref[rows, :] = _dot(h, wv_ref[...]).astype(BF16)
        gate_ref[rows, :] = _dot(h, wg_ref[...]).astype(BF16)
        cs, sn = cs_ref[rows, :], sn_ref[rows, :]
        q = _dot(h, wq_ref[...])
        for hd in range(B_HEADS):
            qh = q[:, hd * B_DK:(hd + 1) * B_DK]
            partner = jnp.where(first, pltpu.roll(qh, B_DK - 32, axis=1), pltpu.roll(qh, 32, axis=1))
            q_ref[rows, hd * B_DK:(hd + 1) * B_DK] = (qh * cs + partner * sn).astype(BF16)
        kt = _nt_dot(wkt_ref[...], h) * (B_DK ** -0.5)
        cst, snt = cst_ref[:, rows], snt_ref[:, rows]
        for hd in range(B_HEADS):
            for half in range(2):
                base = hd * B_DK + half * 64
                x1 = kt[base:base + 32, :]
                x2 = kt[base + 32:base + 64, :]
                c_ = cst[half * 32:(half + 1) * 32, :]
                s_ = snt[half * 32:(half + 1) * 32, :]
                kt_ref[base:base + 32, rows] = (x1 * c_ - x2 * s_).astype(BF16)
                kt_ref[base + 32:base + 64, rows] = (x1 * s_ + x2 * c_).astype(BF16)


def _tok_spec(cols, tm):
    return pl.BlockSpec((None, tm, cols), lambda b, t: (b, t, 0))


def _full_spec(shape):
    nd = len(shape)
    return pl.BlockSpec(shape, lambda b, t: (0,) * nd, pipeline_mode=pl.Buffered(1))


def _mod_specs(d):
    return [pl.BlockSpec((None, N_MOD, d), lambda b, t: (b, 0, 0)), _full_spec((N_MOD, d))]


def _inproj_a(xall, g, mod_l, mod_c, w_in):
    bn, tt, d = xall.shape
    tm = TOK_TILE
    nq, nv = A_HEADS * A_DK, A_HEADS * A_DV
    wq = w_in[:, :nq].astype(BF16)
    wk = w_in[:, nq:2 * nq].astype(BF16)
    wv = w_in[:, 2 * nq:2 * nq + nv].astype(BF16)
    wo = w_in[:, 2 * nq + nv:2 * nq + 2 * nv].astype(BF16)
    wg = w_in[:, 2 * nq + 2 * nv:]
    pad = jnp.zeros((d, 128 - 2 * A_HEADS), F32)
    perm = jnp.array(_GATE_PERM)
    wgi = jnp.concatenate([wg[:, :2 * A_HEADS][:, perm], pad], axis=1).astype(BF16)
    wgf = jnp.concatenate([wg[:, 2 * A_HEADS:][:, perm], pad], axis=1).astype(BF16)
    ws = [wq, wk, wv, wo, wgi, wgf]
    return pl.pallas_call(
        _inproj_a_kernel,
        grid=(bn, tt // tm),
        in_specs=[_tok_spec(d, tm), _full_spec((1, d))] + _mod_specs(d) + [_full_spec(w.shape) for w in ws],
        out_specs=[_tok_spec(nq, tm), _tok_spec(nq, tm),
                   _tok_spec(nv, tm), _tok_spec(nv, tm), _tok_spec(128, tm), _tok_spec(128, tm)],
        out_shape=[jax.ShapeDtypeStruct((bn, tt, nq), BF16), jax.ShapeDtypeStruct((bn, tt, nq), BF16),
                   jax.ShapeDtypeStruct((bn, tt, nv), BF16), jax.ShapeDtypeStruct((bn, tt, nv), BF16),
                   jax.ShapeDtypeStruct((bn, tt, 128), F32), jax.ShapeDtypeStruct((bn, tt, 128), F32)],
        compiler_params=_params("parallel", "parallel"),
        name="inproj_mlstm",
    )(xall, g.reshape(1, d), mod_l, mod_c, *ws)


def _rope_tables(tt):
    seq = tt - CHUNK
    rows = seq // GRID_W
    r = jnp.broadcast_to(jnp.arange(rows, dtype=F32)[:, None], (rows, GRID_W)).reshape(-1)
    col = jnp.broadcast_to(jnp.arange(GRID_W, dtype=F32)[None, :], (rows, GRID_W)).reshape(-1)
    nf = B_DK // 4
    inv = ROPE_BASE ** (-jnp.arange(nf, dtype=F32) / nf)
    ang = jnp.concatenate([r[:, None] * inv, col[:, None] * inv], axis=1)
    cos = jnp.concatenate([jnp.cos(ang), jnp.ones((CHUNK, 2 * nf), F32)], axis=0)
    sin = jnp.concatenate([jnp.sin(ang), jnp.zeros((CHUNK, 2 * nf), F32)], axis=0)
    cr, cc, sr, sc = cos[:, :nf], cos[:, nf:], sin[:, :nf], sin[:, nf:]
    cs = jnp.concatenate([cr, cr, cc, cc], axis=1)
    sn = jnp.concatenate([-sr, sr, -sc, sc], axis=1)
    return cs, sn, cos.T, sin.T


def _inproj_b(xall, g, mod_l, mod_c, w_in):
    bn, tt, d = xall.shape
    tm = TOK_TILE
    n = B_HEADS * B_DK
    wq = w_in[:, :n].astype(BF16)
    wkt = w_in[:, n:2 * n].T.astype(BF16)
    wv = w_in[:, 2 * n:3 * n].astype(BF16)
    wg = w_in[:, 3 * n:].astype(BF16)
    cs, sn, cst, snt = _rope_tables(tt)
    ws = [wq, wkt, wv, wg]
    tab_specs = [pl.BlockSpec((tm, B_DK), lambda b, t: (t, 0)), pl.BlockSpec((tm, B_DK), lambda b, t: (t, 0)),
                 pl.BlockSpec((64, tm), lambda b, t: (0, t)), pl.BlockSpec((64, tm), lambda b, t: (0, t))]
    return pl.pallas_call(
        _inproj_b_kernel,
        grid=(bn, tt // tm),
        in_specs=[_tok_spec(d, tm), _full_spec((1, d))] + _mod_specs(d) + tab_specs
                 + [_full_spec(w.shape) for w in ws],
        out_specs=[_tok_spec(n, tm), pl.BlockSpec((None, n, tm), lambda b, t: (b, 0, t)),
                   _tok_spec(n, tm), _tok_spec(n, tm)],
        out_shape=[jax.ShapeDtypeStruct((bn, tt, n), BF16), jax.ShapeDtypeStruct((bn, n, tt), BF16),
                   jax.ShapeDtypeStruct((bn, tt, n), BF16), jax.ShapeDtypeStruct((bn, tt, n), BF16)],
        compiler_params=_params("parallel", "parallel"),
        name="inproj_ret",
    )(xall, g.reshape(1, d), mod_l, mod_c, cs, sn, cst, snt, *ws)


_GATE_PERM = [d * A_HEADS + 2 * hp + hh for hp in range(A_HEADS // 2) for d in range(2) for hh in range(2)]


def _scan_chunks(tt, chunk, reverse):
    n = tt // chunk
    nctx = CHUNK // chunk
    order = list(range(n - nctx, n)) + list(range(n - nctx))
    if reverse:
        order = list(range(n - 1, n - nctx - 1, -1)) + list(range(n - nctx - 1, -1, -1))
    return order


def _chunk_scan(x, pos, op, reverse):
    n = x.shape[0]
    s = 1
    while s < CHUNK:
        if reverse:
            sh = pltpu.roll(x, n - s, axis=0)
            ok = pos < CHUNK - s
        else:
            sh = pltpu.roll(x, s, axis=0)
            ok = pos >= s
        x = jnp.where(ok, op(x, sh), x)
        s *= 2
    return x


def _mlstm_gates_kernel(gi_ref, gf_ref, bi_ref, bf_ref, b_ref, m_ref, at_ref):
    ig = gi_ref[...] + bi_ref[...]
    lf = _log_sigmoid(gf_ref[...] + bf_ref[...])
    tt = ig.shape[0]
    pos = lax.broadcasted_iota(I32, (tt, 1), 0) % CHUNK
    fwd = (lax.broadcasted_iota(I32, (1, ig.shape[1]), 1) // 2) % 2 == 0
    pre = _chunk_scan(lf, pos, jnp.add, False)
    suf = jnp.concatenate([pre[c * CHUNK + CHUNK - 1:(c + 1) * CHUNK, :] - pre[c * CHUNK:(c + 1) * CHUNK, :]
                           for c in range(tt // CHUNK)], axis=0) + lf
    b = jnp.where(fwd, pre, suf)
    a = ig - b
    mloc = jnp.where(fwd, _chunk_scan(a, pos, jnp.maximum, False), _chunk_scan(a, pos, jnp.maximum, True))
    for hp in range(A_HEADS // 2):
        b_ref[hp] = b[:, 4 * hp:4 * hp + 4]
        m_ref[hp] = mloc[:, 4 * hp:4 * hp + 4]
    at_ref[...] = a.T[:2 * A_HEADS, :]


def _mlstm_gates(gi, gf, ig_b, fg_b):
    bn, tt, w = gi.shape
    nc = 2 * A_HEADS
    perm = jnp.array(_GATE_PERM)
    padrow = lambda v: jnp.concatenate([v.reshape(nc)[perm].reshape(1, nc).astype(F32),
                                        jnp.zeros((1, w - nc), F32)], axis=1)
    nhp = A_HEADS // 2
    spec = pl.BlockSpec((None, tt, w), lambda b: (b, 0, 0))
    row = pl.BlockSpec((1, w), lambda b: (0, 0))
    col_spec = pl.BlockSpec((None, nhp, tt, 4), lambda b: (b, 0, 0, 0))
    row_spec = pl.BlockSpec((None, nc, tt), lambda b: (b, 0, 0))
    col_shape = jax.ShapeDtypeStruct((bn, nhp, tt, 4), F32)
    row_shape = jax.ShapeDtypeStruct((bn, nc, tt), F32)
    return pl.pallas_call(
        _mlstm_gates_kernel,
        grid=(bn,),
        in_specs=[spec, spec, row, row],
        out_specs=[col_spec, col_spec, row_spec],
        out_shape=[col_shape, col_shape, row_shape],
        compiler_params=_params("parallel"),
        name="mlstm_gates",
    )(gi, gf, padrow(ig_b), padrow(fg_b))


CONV_PAD = 8


def _mlstm_scan_kernel(q_ref, k_ref, v_ref, o_ref, b_ref, m_ref, at_ref, cwq_ref, cbq_ref, cwk_ref, cbk_ref,
                       ng_ref, y_ref, xq_ref, xk_ref, qs_ref, ks_ref, hs_ref):
    tt = q_ref.shape[0]
    seq = tt - CHUNK
    half = CONV_W // 2
    zeros = jnp.zeros((CONV_PAD, q_ref.shape[1]), F32)
    bases = []
    for c in range(tt // CHUNK):
        r0 = c * CHUNK
        base = CONV_PAD + r0 + (2 * CONV_PAD if r0 >= seq else 0)
        bases.append(base)
        xq_ref[base:base + CHUNK, :] = q_ref[r0:r0 + CHUNK, :].astype(F32)
        xk_ref[base:base + CHUNK, :] = k_ref[r0:r0 + CHUNK, :].astype(F32)
    for x_ref in (xq_ref, xk_ref):
        x_ref[0:CONV_PAD, :] = zeros
        x_ref[CONV_PAD + seq:3 * CONV_PAD + seq, :] = jnp.concatenate([zeros, zeros], axis=0)
        x_ref[3 * CONV_PAD + tt:4 * CONV_PAD + tt, :] = zeros
    for c, base in enumerate(bases):
        rows = slice(c * CHUNK, (c + 1) * CHUNK)
        qa = cbq_ref[...]
        ka = cbk_ref[...]
        for w in range(CONV_W):
            qa = qa + xq_ref[base + w - half:base + w - half + CHUNK, :] * cwq_ref[w:w + 1, :]
            ka = ka + xk_ref[base + w - half:base + w - half + CHUNK, :] * cwk_ref[w:w + 1, :]
        qs_ref[rows, :] = (qa * _sigmoid(qa) * (A_DK ** -0.5)).astype(BF16)
        ks_ref[:, rows] = (ka * _sigmoid(ka)).T.astype(BF16)

    ii = lax.broadcasted_iota(I32, (CHUNK, CHUNK), 0)
    jj = lax.broadcasted_iota(I32, (CHUNK, CHUNK), 1)
    ones_blk = jnp.ones((CHUNK, A_DV), BF16)

    for hh in range(2):
        for d in range(2):
            mask = (jj >= ii) if d else (jj <= ii)
            last = 0 if d else CHUNK - 1
            state = jnp.zeros((A_DK, 2 * A_DV), F32)
            m = jnp.zeros((1, 1), F32)
            for c in _scan_chunks(tt, CHUNK, bool(d)):
                r0 = c * CHUNK
                qc = qs_ref[r0:r0 + CHUNK, hh * A_DK:(hh + 1) * A_DK]
                ktc = ks_ref[hh * A_DK:(hh + 1) * A_DK, r0:r0 + CHUNK]
                vaug = jnp.concatenate([v_ref[r0:r0 + CHUNK, hh * A_DV:(hh + 1) * A_DV], ones_blk], axis=1)
                col = slice(2 * d + hh, 2 * d + hh + 1)
                a_row = at_ref[col, r0:r0 + CHUNK]
                mcol = jnp.maximum(m, m_ref[r0:r0 + CHUNK, col])
                bcol = b_ref[r0:r0 + CHUNK, col]
                decay = jnp.where(mask, jnp.exp(a_row - mcol), 0.0)
                s = (_dot(qc, ktc) * decay).astype(BF16)
                qw = (qc.astype(F32) * jnp.exp(m - mcol)).astype(BF16)
                nd = _dot(jnp.concatenate([s, qw], axis=1), jnp.concatenate([vaug, state.astype(BF16)], axis=0))
                hout = nd[:, :A_DV] / jnp.maximum(jnp.abs(nd[:, A_DV:]), jnp.exp(-(bcol + mcol)))
                dst = (slice(r0, r0 + CHUNK), slice(hh * A_DV, (hh + 1) * A_DV))
                if d:
                    hs_ref[dst] = hs_ref[dst] + hout
                else:
                    hs_ref[dst] = hout
                m_last = mcol[last:last + 1, :]
                wk = jnp.exp(a_row - m_last)
                state = jnp.exp(m - m_last) * state + _dot((ktc.astype(F32) * wk).astype(BF16), vaug)
                m = bcol[last:last + 1, :] + m_last
    for c in range(tt // CHUNK):
        rows = slice(c * CHUNK, (c + 1) * CHUNK)
        for hh in range(2):
            sl = slice(hh * A_DV, (hh + 1) * A_DV)
            y = hs_ref[rows, sl]
            y = y * lax.rsqrt(jnp.mean(y * y, axis=-1, keepdims=True) + EPS)
            y = y * ng_ref[:, sl] * _sigmoid(o_ref[rows, sl].astype(F32))
            y_ref[rows, sl] = y.astype(BF16)


def _mlstm_scan(q, k, v, o, bcs, mloc, at, conv_w, conv_b, norm_g):
    bn, tt, nq = q.shape
    nv = v.shape[-1]
    nhp = A_HEADS // 2
    col_spec = pl.BlockSpec((None, None, tt, 4), lambda b, h: (b, h, 0, 0))
    row_spec = pl.BlockSpec((None, None, 4, tt), lambda b, h: (b, h, 0, 0))
    cwq = conv_w[:, :nq]
    cwk = conv_w[:, nq:]
    cbq = conv_b[:nq].reshape(1, nq)
    cbk = conv_b[nq:].reshape(1, nq)
    return pl.pallas_call(
        _mlstm_scan_kernel,
        grid=(bn, A_HEADS // 2),
        in_specs=[pl.BlockSpec((None, tt, 2 * A_DK), lambda b, h: (b, 0, h)),
                  pl.BlockSpec((None, tt, 2 * A_DK), lambda b, h: (b, 0, h)),
                  pl.BlockSpec((None, tt, 2 * A_DV), lambda b, h: (b, 0, h)),
                  pl.BlockSpec((None, tt, 2 * A_DV), lambda b, h: (b, 0, h))]
                 + [col_spec, col_spec, row_spec] + [
                  pl.BlockSpec((CONV_W, 2 * A_DK), lambda b, h: (0, h)),
                  pl.BlockSpec((1, 2 * A_DK), lambda b, h: (0, h)),
                  pl.BlockSpec((CONV_W, 2 * A_DK), lambda b, h: (0, h)),
                  pl.BlockSpec((1, 2 * A_DK), lambda b, h: (0, h)),
                  pl.BlockSpec((1, 2 * A_DV), lambda b, h: (0, h))],
        out_specs=pl.BlockSpec((None, tt, 2 * A_DV), lambda b, h: (b, 0, h)),
        out_shape=jax.ShapeDtypeStruct((bn, tt, nv), BF16),
        scratch_shapes=[pltpu.VMEM((tt + 4 * CONV_PAD, 2 * A_DK), F32), pltpu.VMEM((tt + 4 * CONV_PAD, 2 * A_DK), F32),
                        pltpu.VMEM((tt, 2 * A_DK), BF16), pltpu.VMEM((2 * A_DK, tt), BF16),
                        pltpu.VMEM((tt, 2 * A_DV), F32)],
        compiler_params=_params("parallel", "parallel"),
        name="mlstm_scan",
    )(q, k, v, o, bcs, mloc, at.reshape(bn, nhp, 4, tt), cwq, cbq, cwk, cbk, norm_g.reshape(1, nv))


RET_HEADS = 2


def _ret_scan_kernel(q_ref, kt_ref, v_ref, g_ref, dl_ref, gn_ref, y_ref, st_ref):
    tt = q_ref.shape[0]
    L = CHUNK
    ii = lax.broadcasted_iota(I32, (L, L), 0)
    jj = lax.broadcasted_iota(I32, (L, L), 1)
    ci = lax.broadcasted_iota(I32, (L, 1), 0).astype(F32)
    rj = lax.broadcasted_iota(I32, (1, L), 1).astype(F32)
    dist = (ii - jj).astype(F32)
    for hh in range(RET_HEADS):
        cols = slice(hh * B_DK, (hh + 1) * B_DK)
        logg = [_log_sigmoid(dl_ref[0, hh]), _log_sigmoid(dl_ref[1, hh])]
        for d in range(2):
            order = _scan_chunks(tt, L, bool(d))
            kdec = jnp.exp((rj if d else L - 1.0 - rj) * logg[d])
            sdec = jnp.exp(L * logg[d])
            state = jnp.zeros((B_DK, B_DV), F32)
            for n, c in enumerate(order):
                rows = slice(c * L, (c + 1) * L)
                st_ref[hh, d, c] = state.astype(BF16)
                if n + 1 < len(order):
                    state = sdec * state + _dot((kt_ref[cols, rows].astype(F32) * kdec).astype(BF16), v_ref[rows, cols])
        dcomb = (jnp.where(ii >= jj, jnp.exp(dist * logg[0]), 0.0)
                 + jnp.where(jj >= ii, jnp.exp(-dist * logg[1]), 0.0))
        inter_f = jnp.exp((ci + 1.0) * logg[0])
        inter_b = jnp.exp((L - ci) * logg[1])
        for c in range(tt // L):
            rows = slice(c * L, (c + 1) * L)
            qc = q_ref[rows, cols]
            q32 = qc.astype(F32)
            s = (_dot(qc, kt_ref[cols, rows]) * dcomb).astype(BF16)
            lhs = jnp.concatenate([s, (q32 * inter_f).astype(BF16), (q32 * inter_b).astype(BF16)], axis=1)
            rhs = jnp.concatenate([v_ref[rows, cols], st_ref[hh, 0, c], st_ref[hh, 1, c]], axis=0)
            y = _dot(lhs, rhs)
            mu = jnp.mean(y, axis=-1, keepdims=True)
            yc = y - mu
            var = jnp.mean(yc * yc, axis=-1, keepdims=True)
            g = g_ref[rows, cols].astype(F32)
            y_ref[rows, cols] = (yc * lax.rsqrt(var + EPS) * gn_ref[:, cols] * (g * _sigmoid(g))).astype(BF16)


def _ret_scan(q, kt, v, g, decay_logit, gn_g):
    bn, tt, n = q.shape
    nh = RET_HEADS
    dl = decay_logit.astype(F32).reshape(2, B_HEADS // nh, nh, 1, 1).transpose(1, 0, 2, 3, 4)
    tok = pl.BlockSpec((None, tt, nh * B_DK), lambda b, h: (b, 0, h))
    return pl.pallas_call(
        _ret_scan_kernel,
        grid=(bn, B_HEADS // nh),
        in_specs=[tok, pl.BlockSpec((None, nh * B_DK, tt), lambda b, h: (b, h, 0)), tok, tok,
                  pl.BlockSpec((None, 2, nh, 1, 1), lambda b, h: (h, 0, 0, 0, 0)),
                  pl.BlockSpec((1, nh * B_DV), lambda b, h: (0, h))],
        out_specs=tok,
        out_shape=jax.ShapeDtypeStruct((bn, tt, n), BF16),
        scratch_shapes=[pltpu.VMEM((nh, 2, tt // CHUNK, B_DK, B_DV), BF16)],
        compiler_params=_params("parallel", "parallel"),
        name="ret_scan",
    )(q, kt, v, g, dl, gn_g.reshape(1, n))


def _outproj_kernel(y_ref, x_ref, w_ref, ml_ref, mc_ref, g_ref, r2_ref, x1_ref, h2_ref, lg_ref):
    for s in range(x_ref.shape[0] // CHUNK):
        rows = slice(s * CHUNK, (s + 1) * CHUNK)
        mod = _tile_mod(ml_ref, mc_ref, pl.program_id(1), pl.num_programs(1), x_ref.shape[0], s)
        out = _dot(y_ref[rows, :], w_ref[...])
        x1 = x_ref[rows, :] + mod[2:3, :] * out
        x1_ref[rows, :] = x1
        h2 = _rms_mod(x1, g_ref[...], mod[3:4, :], mod[4:5, :])
        hi = h2.astype(BF16)
        h2_ref[rows, :] = hi
        lo = (h2 - hi.astype(F32)).astype(BF16)
        a = _dot(hi, r2_ref[...])
        lg_ref[rows, :] = (a[:, :LANES] + a[:, LANES:]) + _dot(lo, r2_ref[:, :LANES])


def _outproj(y, xall, w_out, mod_l, mod_c, g, router):
    bn, tt, d = xall.shape
    tm = TOK_TILE
    ne = router.shape[1]
    r = jnp.concatenate([router.astype(F32), jnp.zeros((d, LANES - ne), F32)], axis=1)
    rhi = r.astype(BF16)
    r2 = jnp.concatenate([rhi, (r - rhi.astype(F32)).astype(BF16)], axis=1)
    return pl.pallas_call(
        _outproj_kernel,
        grid=(bn, tt // tm),
        in_specs=[_tok_spec(y.shape[-1], tm), _tok_spec(d, tm), _full_spec(w_out.shape)] + _mod_specs(d)
                 + [_full_spec((1, d)), _full_spec((d, 2 * LANES))],
        out_specs=[_tok_spec(d, tm), _tok_spec(d, tm), _tok_spec(LANES, tm)],
        out_shape=[jax.ShapeDtypeStruct((bn, tt, d), F32), jax.ShapeDtypeStruct((bn, tt, d), BF16),
                   jax.ShapeDtypeStruct((bn, tt, LANES), F32)],
        compiler_params=_params("parallel", "parallel"),
        name="outproj_router",
    )(y, xall, w_out.astype(BF16), mod_l, mod_c, g.reshape(1, d), r2)


def _lane_cumsum(x, upper):
    n = x.shape[1]
    parts = []
    carry = jnp.zeros((x.shape[0], 1), F32)
    for j in range(n // CHUNK):
        cs = _dot(x[:, j * CHUNK:(j + 1) * CHUNK].astype(BF16), upper) + carry
        parts.append(cs)
        carry = cs[:, CHUNK - 1:CHUNK]
    return parts[0] if len(parts) == 1 else jnp.concatenate(parts, axis=1)


def _kth_largest(vals, caps):
    def count(x, cand_bits):
        return jnp.sum((x >= pltpu.bitcast(cand_bits, F32)).astype(F32), axis=1, keepdims=True)

    taus = [jnp.zeros((x.shape[0], 1), I32) for x in vals]
    for p, nbits in [(30, 1)] + [(p, 2) for p in range(28, -1, -2)]:
        for i, (x, cap) in enumerate(zip(vals, caps)):
            tau = taus[i]
            digit = sum((count(x, tau + (j << p)) >= cap).astype(I32) for j in range(1, 1 << nbits))
            taus[i] = tau + digit * (1 << p)
    return [pltpu.bitcast(tau, F32) for tau in taus]


ROUTE_BATCH = 4


def _route_kernel(lg_ref, pos_ref, post_ref, gatet_ref, gate_ref, *, segs):
    nb, ne = pos_ref.shape[0], pos_ref.shape[1]
    si = lax.broadcasted_iota(I32, (CHUNK, CHUNK), 0)
    ti = lax.broadcasted_iota(I32, (CHUNK, CHUNK), 1)
    upper = (si <= ti).astype(BF16)
    work = []
    for i in range(nb):
        lgt = lg_ref[i].T[:ne, :]
        for seg in segs:
            s0, n, cap, tag = seg
            lg = lgt[:, s0:s0 + n]
            e = jnp.exp(lg - jnp.max(lg, axis=0, keepdims=True))
            work.append((i, seg, e / jnp.sum(e, axis=0, keepdims=True)))
    taus = _kth_largest([aff for (_, _, aff) in work], [seg[2] for (_, seg, _) in work])
    for (i, (s0, n, cap, tag), aff), tau in zip(work, taus):
        gt = aff > tau
        eq = aff == tau
        need = cap - jnp.sum(gt.astype(F32), axis=1, keepdims=True)
        cum_eq = _lane_cumsum(eq.astype(F32), upper)
        sel = gt | (eq & (cum_eq <= need))
        pos = _lane_cumsum(sel.astype(F32), upper) - 1.0 + tag
        pos_ref[i, :, s0:s0 + n] = jnp.where(sel, pos, -1.0)
        gate_ref[i, :, s0:s0 + n] = aff
    pad = jnp.zeros((LANES - ne, pos_ref.shape[2]), F32)
    for i in range(nb):
        post_ref[i] = jnp.concatenate([pos_ref[i], pad], axis=0).T[:, :ne]
        gatet_ref[i] = jnp.concatenate([gate_ref[i], pad], axis=0).T[:, :ne]


def _route(logits, ne, segs):
    bn, tt, w = logits.shape
    nb = ROUTE_BATCH
    row = pl.BlockSpec((nb, ne, tt), lambda b: (b, 0, 0))
    col = pl.BlockSpec((nb, tt, ne), lambda b: (b, 0, 0))
    return pl.pallas_call(
        functools.partial(_route_kernel, segs=segs),
        grid=(bn // nb,),
        in_specs=[pl.BlockSpec((nb, tt, w), lambda b: (b, 0, 0))],
        out_specs=[row, col, col],
        out_shape=[jax.ShapeDtypeStruct((bn, ne, tt), F32),
                   jax.ShapeDtypeStruct((bn, tt, ne), F32), jax.ShapeDtypeStruct((bn, tt, ne), F32)],
        scratch_shapes=[pltpu.VMEM((nb, ne, tt), F32)],
        compiler_params=_params("parallel"),
        name="route",
    )(logits)


EXP_GROUP = 8


def _onehot_rows(pos_ref, lo, hi, cap, tag):
    parts = []
    slot = lax.broadcasted_iota(I32, (cap, hi - lo), 0).astype(F32) + tag
    for r in range(EXP_GROUP):
        parts.append(jnp.where(pos_ref[r:r + 1, lo:hi] == slot, 1.0, 0.0).astype(BF16))
    return jnp.concatenate(parts, axis=0)


def _gather_kernel(pos_ref, h_ref, xl_ref, *maybe_xc_ref, cap_c, cap_l):
    tt = h_ref.shape[0]
    seq = tt - CHUNK
    xl_ref[...] = _dot(_onehot_rows(pos_ref, 0, seq, cap_l, 0.0), h_ref[0:seq, :]).astype(BF16)
    for xc_ref in maybe_xc_ref:
        xc_ref[...] = _dot(_onehot_rows(pos_ref, seq, tt, cap_c, float(cap_l)), h_ref[seq:tt, :]).astype(BF16)


def _gather(pos, h2, cap_c, cap_l, with_ctx):
    bn, ne, tt = pos.shape
    d = h2.shape[-1]
    ng = ne // EXP_GROUP
    pos4 = pos.reshape(bn, ng, EXP_GROUP, tt)
    caps = [cap_l] + ([cap_c] if with_ctx else [])
    return pl.pallas_call(
        functools.partial(_gather_kernel, cap_c=cap_c, cap_l=cap_l),
        grid=(bn, ng),
        in_specs=[pl.BlockSpec((None, None, EXP_GROUP, tt), lambda b, g: (b, g, 0, 0)),
                  pl.BlockSpec((None, tt, d), lambda b, g: (b, 0, 0))],
        out_specs=[pl.BlockSpec((None, EXP_GROUP * cap, d), lambda b, g: (b, g, 0)) for cap in caps],
        out_shape=[jax.ShapeDtypeStruct((bn, ne * cap, d), BF16) for cap in caps],
        compiler_params=_params("parallel", "arbitrary"),
        name="moe_gather",
    )(pos4, h2)


FFN_BATCH = 4


def _ffn_kernel(*refs, nseg):
    x_refs = refs[:nseg]
    wg_ref, wu_ref, wd_ref = refs[nseg:nseg + 3]
    y_refs = refs[nseg + 3:2 * nseg + 3]
    wg16, wu16, wd16 = refs[2 * nseg + 3:]

    @pl.when(pl.program_id(1) == 0)
    def _():
        wg16[...] = wg_ref[...].astype(BF16)
        wu16[...] = wu_ref[...].astype(BF16)
        wd16[...] = wd_ref[...].astype(BF16)

    d = x_refs[0].shape[-1]
    rows = [r.shape[0] * r.shape[1] for r in x_refs]
    xs = [r[...].reshape(n, d) for r, n in zip(x_refs, rows)]
    x = xs[0] if nseg == 1 else jnp.concatenate(xs, axis=0)
    g = _dot(x, wg16[...])
    u = _dot(x, wu16[...])
    hid = (g * _sigmoid(g) * u).astype(BF16)
    y = _dot(hid, wd16[...]).astype(BF16)
    r0 = 0
    for y_ref, n in zip(y_refs, rows):
        y_ref[...] = y[r0:r0 + n].reshape(y_ref.shape)
        r0 += n


def _ffn(xs, caps, w_gate, w_up, w_down, layer):
    bn = xs[0].shape[0]
    _, ne, d, f = w_gate.shape
    bb = FFN_BATCH
    x4 = [x.reshape(bn, ne, cap, d) for x, cap in zip(xs, caps)]
    specs = [pl.BlockSpec((bb, None, cap, d), lambda e, b: (b, e, 0, 0)) for cap in caps]
    wspec = lambda s: pl.BlockSpec((None, None) + s, lambda e, b: (layer, e, 0, 0))
    ys = pl.pallas_call(
        functools.partial(_ffn_kernel, nseg=len(xs)),
        grid=(ne, bn // bb),
        in_specs=specs + [wspec((d, f)), wspec((d, f)), wspec((f, d))],
        out_specs=specs,
        out_shape=[jax.ShapeDtypeStruct(x.shape, BF16) for x in x4],
        scratch_shapes=[pltpu.VMEM((d, f), BF16), pltpu.VMEM((d, f), BF16), pltpu.VMEM((f, d), BF16)],
        compiler_params=_params("parallel", "arbitrary"),
        name="moe_ffn",
    )(*x4, w_gate, w_up, w_down)
    return [y.reshape(x.shape) for y, x in zip(ys, xs)]


SCATTER_TILE = 1024


def _onehot_cols(post_ref, gatet_ref, cap, tag):
    nrows, ne = post_ref.shape
    slot = lax.broadcasted_iota(I32, (nrows, cap), 1).astype(F32) + tag
    parts = [jnp.where(post_ref[:, e:e + 1] == slot, gatet_ref[:, e:e + 1], 0.0).astype(BF16) for e in range(ne)]
    return jnp.concatenate(parts, axis=1)


def _scatter_lat_kernel(post_ref, gatet_ref, yl_ref, x_ref, ml_ref, fg_ref, o_ref, *, cap, final):
    moe = _dot(_onehot_cols(post_ref, gatet_ref, cap, 0.0), yl_ref[...])
    x2 = x_ref[...] + ml_ref[5:6, :] * moe
    if final:
        x2 = x2 * lax.rsqrt(jnp.mean(x2 * x2, axis=-1, keepdims=True) + EPS) * fg_ref[...]
    o_ref[...] = x2


def _scatter_ctx_kernel(post_ref, gatet_ref, yc_ref, x_ref, mc_ref, o_ref, *, cap, tag):
    moe = _dot(_onehot_cols(post_ref, gatet_ref, cap, tag), yc_ref[...])
    o_ref[...] = x_ref[...] + mc_ref[5:6, :] * moe


def _scatter_lat(post, gatet, yl, x1, mod_l, final_g, cap, final):
    bn, tt, d = x1.shape
    ne = post.shape[-1]
    seq = tt - CHUNK
    tm = SCATTER_TILE
    tok = lambda cols: pl.BlockSpec((None, tm, cols), lambda b, t: (b, t, 0))
    return pl.pallas_call(
        functools.partial(_scatter_lat_kernel, cap=cap, final=final),
        grid=(bn, seq // tm),
        in_specs=[tok(ne), tok(ne), pl.BlockSpec((None,) + yl.shape[1:], lambda b, t: (b, 0, 0)), tok(d),
                  pl.BlockSpec((None, N_MOD, d), lambda b, t: (b, 0, 0)), _full_spec((1, d))],
        out_specs=tok(d),
        out_shape=jax.ShapeDtypeStruct((bn, seq if final else tt, d), F32),
        input_output_aliases={} if final else {3: 0},
        compiler_params=_params("parallel", "arbitrary"),
        name="moe_scatter_final" if final else "moe_scatter_lat",
    )(post, gatet, yl, x1, mod_l, final_g.reshape(1, d))


def _scatter_ctx(post, gatet, yc, x1, mod_c, cap, tag):
    bn, tt, d = x1.shape
    ne = post.shape[-1]
    cblk = tt // CHUNK - 1
    tok = lambda cols: pl.BlockSpec((None, CHUNK, cols), lambda b: (b, cblk, 0))
    return pl.pallas_call(
        functools.partial(_scatter_ctx_kernel, cap=cap, tag=tag),
        grid=(bn,),
        in_specs=[tok(ne), tok(ne), pl.BlockSpec((None,) + yc.shape[1:], lambda b: (b, 0, 0)), tok(d),
                  pl.BlockSpec((N_MOD, d), lambda b: (0, 0))],
        out_specs=tok(d),
        out_shape=jax.ShapeDtypeStruct((bn, tt, d), F32),
        input_output_aliases={3: 0},
        compiler_params=_params("parallel"),
        name="moe_scatter_ctx",
    )(post, gatet, yc, x1, mod_c)


def kernel(x, c, ctx, c_ctx, ada_w, ada_b, norm_mix_g, norm_ffn_g, final_norm_g, mlstm_w_in, mlstm_conv_w,
           mlstm_conv_b, mlstm_igate_b, mlstm_fgate_b, mlstm_head_norm_g, mlstm_w_out, ret_w_in, ret_decay_logit,
           ret_group_norm_g, ret_w_out, moe_router, moe_w_gate, moe_w_up, moe_w_down):
    bn, seq, d = x.shape
    depth = ada_w.shape[0]
    ne = moe_router.shape[-1]
    assert ctx.shape[1] == CHUNK and seq % SCATTER_TILE == 0 and (seq + CHUNK) % TOK_TILE == 0
    cap_c = CAPACITY_FACTOR * CHUNK // ne
    cap_l = CAPACITY_FACTOR * seq // ne
    assert cap_l % LANES == 0
    segs = ((0, seq, cap_l, 0.0), (seq, CHUNK, cap_c, float(cap_l)))

    rows = -(-(bn + 1) // 8) * 8
    cvec = jnp.concatenate([c, c_ctx[None, :], jnp.zeros((rows - bn - 1, d), F32)], axis=0)
    mod = _adaln(cvec, ada_w, ada_b)
    xall = jnp.concatenate([x, ctx], axis=1)

    for i in range(depth):
        last = i == depth - 1
        mod_l = mod[i, :bn].reshape(bn, N_MOD, d)
        mod_c = mod[i, bn].reshape(N_MOD, d)
        j = i // 2
        if i % 2 == 0:
            q, kt, v, o, gi, gf = _inproj_a(xall, norm_mix_g[i], mod_l, mod_c, mlstm_w_in[j])
            bcs, mloc, at = _mlstm_gates(gi, gf, mlstm_igate_b[j], mlstm_fgate_b[j])
            y = _mlstm_scan(q, kt, v, o, bcs, mloc, at, mlstm_conv_w[j], mlstm_conv_b[j], mlstm_head_norm_g[j])
            w_out = mlstm_w_out[j]
        else:
            q, kt, v, g = _inproj_b(xall, norm_mix_g[i], mod_l, mod_c, ret_w_in[j])
            y = _ret_scan(q, kt, v, g, ret_decay_logit[j], ret_group_norm_g[j])
            w_out = ret_w_out[j]
        x1, h2, logits = _outproj(y, xall, w_out, mod_l, mod_c, norm_ffn_g[i], moe_router[i])
        pos, post, gatet = _route(logits, ne, segs)
        caps = [cap_l] if last else [cap_l, cap_c]
        xs = _gather(pos, h2, cap_c, cap_l, not last)
        ys = _ffn(xs, caps, moe_w_gate, moe_w_up, moe_w_down, i)
        xall = _scatter_lat(post, gatet, ys[0], x1, mod_l, final_norm_g, cap_l, last)
        if not last:
            xall = _scatter_ctx(post, gatet, ys[1], xall, mod_c, cap_c, float(cap_l))
    return xall
```
